```python
import math
import jax, jax.numpy as jnp
from jax import lax
import numpy as np

D_MODEL = 2048
BATCH = 8
SEQ = 2048
DEPTH = 4

HEAD_DIM = 128
A_PATTERNS = ((128, 1), (512, 4), (2048, 16))
A_GROUPS = len(A_PATTERNS)
A_HEADS_PER_GROUP = 4
A_HEADS = A_GROUPS * A_HEADS_PER_GROUP
B_HEADS = 8
N_A = 3 * A_HEADS * HEAD_DIM
N_B = 3 * B_HEADS * HEAD_DIM
N_IN = N_A + N_B + B_HEADS + 2 * D_MODEL
A_OUT = A_HEADS_PER_GROUP * HEAD_DIM
B_OUT = B_HEADS * HEAD_DIM
N_BUCKETS = 32
REL_MAX_DIST = 2048
D_FF = 5632
CONV_WIDTH = 3
Q_BLOCK = 128
EPS = 1e-6
NEG = -1e30

kernel_name = "hybrid_gated_dilated_fox_convffn"


def rms_norm(x, g):
    xf = x.astype(jnp.float32)
    y = xf * lax.rsqrt(jnp.mean(xf * xf, axis=-1, keepdims=True) + EPS)
    return (y * g.astype(jnp.float32)).astype(x.dtype)


def rel_bucket(dist):
    max_exact = N_BUCKETS // 2
    nf = jnp.maximum(dist, 1).astype(jnp.float32)
    large = max_exact + (jnp.log(nf / max_exact) / math.log(REL_MAX_DIST / max_exact)
                         * (N_BUCKETS - max_exact)).astype(jnp.int32)
    large = jnp.minimum(large, N_BUCKETS - 1)
    return jnp.where(dist < max_exact, dist, large)


def dilated_group(q, k, v, bias_table, window, dilation):
    b, t, h, hd = q.shape
    length = t // dilation
    span = window // dilation
    qb = min(Q_BLOCK, length)
    n_prev = -(-span // qb)
    nb = -(-length // qb)
    lp = nb * qb

    def to_strided(a):
        return a.reshape(b, length, dilation, h, hd).transpose(0, 2, 3, 1, 4)

    qs, ks, vs = to_strided(q), to_strided(k), to_strided(v)
    qs = jnp.pad(qs, [(0, 0)] * 3 + [(0, lp - length), (0, 0)]).reshape(b, dilation, h, nb, qb, hd)
    pad_kv = [(0, 0)] * 3 + [(n_prev * qb, lp - length), (0, 0)]
    ks = jnp.pad(ks, pad_kv).reshape(b, dilation, h, nb + n_prev, qb, hd)
    vs = jnp.pad(vs, pad_kv).reshape(b, dilation, h, nb + n_prev, qb, hd)
    kw = jnp.concatenate([ks[:, :, :, j:j + nb] for j in range(n_prev + 1)], axis=4)
    vw = jnp.concatenate([vs[:, :, :, j:j + nb] for j in range(n_prev + 1)], axis=4)
    kwidth = (n_prev + 1) * qb

    delta = jnp.arange(qb)[:, None] + n_prev * qb - jnp.arange(kwidth)[None, :]
    key_pos = (jnp.arange(nb)[:, None] - n_prev) * qb + jnp.arange(kwidth)[None, :]
    valid = ((delta >= 0) & (delta <= span))[None] & (key_pos >= 0)[:, None, :]
    bias = bias_table[rel_bucket(jnp.maximum(delta, 0) * dilation)].transpose(2, 0, 1)

    s = jnp.einsum('brhnqd,brhnkd->brhnqk', qs, kw).astype(jnp.float32) * (HEAD_DIM ** -0.5)
    s = jnp.where(valid, s + bias[:, None].astype(jnp.float32), NEG)
    m = jnp.max(s, axis=-1, keepdims=True)
    p = jnp.exp(s - m)
    l = jnp.sum(p, axis=-1)
    o = jnp.einsum('brhnqk,brhnkd->brhnqd', p.astype(vw.dtype), vw).astype(jnp.float32) / l[..., None]

    o = o.reshape(b, dilation, h, lp, hd)[:, :, :, :length].transpose(0, 3, 1, 2, 4).reshape(b, t, h, hd)
    m = m[..., 0].reshape(b, dilation, h, lp)[..., :length].transpose(0, 3, 1, 2).reshape(b, t, h)
    l = l.reshape(b, dilation, h, lp)[..., :length].transpose(0, 3, 1, 2).reshape(b, t, h)
    return o, m, l


def dilated_mixture(a_qkv, rel_bias):
    outs, maxes, dens = [], [], []
    for gi, (window, dilation) in enumerate(A_PATTERNS):
        table = rel_bias[:, gi * A_HEADS_PER_GROUP:(gi + 1) * A_HEADS_PER_GROUP]
        o, m, l = dilated_group(a_qkv[:, :, 0, gi], a_qkv[:, :, 1, gi], a_qkv[:, :, 2, gi],
                                table, window, dilation)
        outs.append(o); maxes.append(m); dens.append(l)
    o = jnp.stack(outs)
    m = jnp.stack(maxes)
    l = jnp.stack(dens)
    w = l * jnp.exp(m - jnp.max(m, axis=0, keepdims=True))
    y = jnp.sum(w[..., None] * o, axis=0) / jnp.sum(w, axis=0)[..., None]
    b, t = y.shape[:2]
    return y.reshape(b, t, A_OUT)


def forgetting_attention(q, k, v, log_f):
    b, t, h, hd = q.shape
    nb = t // Q_BLOCK
    c = jnp.cumsum(log_f.astype(jnp.float32), axis=1).transpose(0, 2, 1)
    kt = k.transpose(0, 2, 1, 3)
    vt = v.transpose(0, 2, 1, 3)
    qblk = q.reshape(b, nb, Q_BLOCK, h, hd).transpose(1, 0, 3, 2, 4)
    cblk = c.reshape(b, h, nb, Q_BLOCK).transpose(2, 0, 1, 3)
    kpos = jnp.arange(t)

    def one_block(args):
        qi, ci, i = args
        s = jnp.einsum('bhqd,bhkd->bhqk', qi, kt).astype(jnp.float32) * (HEAD_DIM ** -0.5)
        s = s + ci[..., None] - c[:, :, None, :]
        qpos = i * Q_BLOCK + jnp.arange(Q_BLOCK)
        s = jnp.where(kpos[None, :] <= qpos[:, None], s, NEG)
        p = jax.nn.softmax(s, axis=-1)
        return jnp.einsum('bhqk,bhkd->bhqd', p.astype(vt.dtype), vt)

    o = lax.map(one_block, (qblk, cblk, jnp.arange(nb)))
    return o.transpose(1, 0, 3, 2, 4).reshape(b, t, h * hd)


def conv_ffn(h, w_up, conv_w, conv_b, w_down):
    t = h.shape[1]
    u = h @ w_up
    up = jnp.pad(u, ((0, 0), (CONV_WIDTH - 1, 0), (0, 0)))
    uc = sum(conv_w[j] * up[:, j:j + t] for j in range(CONV_WIDTH)) + conv_b
    gate, val = uc[..., :D_FF], uc[..., D_FF:]
    return (jax.nn.gelu(gate, approximate=True) * val) @ w_down


def setup_inputs(seed: int = 0) -> dict:
    key = jax.random.key(seed)
    ks = jax.random.split(key, 16)
    f32 = jnp.float32
    nrm = lambda k, shape, scale: jax.random.normal(k, shape, f32) * scale
    gain = lambda k: 1.0 + 0.05 * jax.random.normal(k, (DEPTH, D_MODEL), f32)
    return {
        "x": jax.random.normal(ks[0], (BATCH, SEQ, D_MODEL), f32),
        "rel_bias": nrm(ks[1], (N_BUCKETS, A_HEADS), 0.5),
        "w_in": nrm(ks[2], (DEPTH, D_MODEL, N_IN), D_MODEL ** -0.5),
        "b_f": 3.0 + 0.5 * jax.random.normal(ks[3], (DEPTH, B_HEADS), f32),
        "w_pa": nrm(ks[4], (DEPTH, A_OUT, D_MODEL), A_OUT ** -0.5),
        "w_pb": nrm(ks[5], (DEPTH, B_OUT, D_MODEL), B_OUT ** -0.5),
        "w_o": nrm(ks[6], (DEPTH, D_MODEL, D_MODEL), D_MODEL ** -0.5),
        "w_up": nrm(ks[7], (DEPTH, D_MODEL, 2 * D_FF), D_MODEL ** -0.5),
        "conv_w": nrm(ks[8], (DEPTH, CONV_WIDTH, 2 * D_FF), CONV_WIDTH ** -0.5),
        "conv_b": nrm(ks[9], (DEPTH, 2 * D_FF), 0.02),
        "w_down": nrm(ks[10], (DEPTH, D_FF, D_MODEL), D_FF ** -0.5),
        "g_mix_pre": gain(ks[11]),
        "g_mix_post": gain(ks[12]),
        "g_ffn_pre": gain(ks[13]),
        "g_ffn_post": gain(ks[14]),
    }


def reference(x, rel_bias, w_in, b_f, w_pa, w_pb, w_o, w_up, conv_w, conv_b, w_down,
              g_mix_pre, g_mix_post, g_ffn_pre, g_ffn_post):
    b, t, _ = x.shape
    for layer in range(DEPTH):
        h = rms_norm(x, g_mix_pre[layer])
        proj = h @ w_in[layer]
        a_qkv = proj[..., :N_A].reshape(b, t, 3, A_GROUPS, A_HEADS_PER_GROUP, HEAD_DIM)
        off = N_A
        b_qkv = proj[..., off:off + N_B].reshape(b, t, 3, B_HEADS, HEAD_DIM)
        off += N_B
        f_logit = proj[..., off:off + B_HEADS]
        off += B_HEADS
        gates = jax.nn.sigmoid(proj[..., off:].astype(jnp.float32)).reshape(b, t, 2, D_MODEL)

        y_a = dilated_mixture(a_qkv, rel_bias).astype(x.dtype)
        log_f = jax.nn.log_sigmoid((f_logit + b_f[layer]).astype(jnp.float32))
        y_b = forgetting_attention(b_qkv[:, :, 0], b_qkv[:, :, 1], b_qkv[:, :, 2], log_f)

        merged = (gates[:, :, 0] * (y_a @ w_pa[layer]).astype(jnp.float32)
                  + gates[:, :, 1] * (y_b @ w_pb[layer]).astype(jnp.float32)).astype(x.dtype)
        x = x + rms_norm(merged @ w_o[layer], g_mix_post[layer])

        h = rms_norm(x, g_ffn_pre[layer])
        y = conv_ffn(h, w_up[layer], conv_w[layer], conv_b[layer], w_down[layer])
        x = x + rms_norm(y, g_ffn_post[layer])
    return x
```

```python
import functools
import math

import jax
import jax.numpy as jnp
from jax import lax
from jax.experimental import pallas as pl
from jax.experimental.pallas import tpu as pltpu

HEAD_DIM = 128
A_PATTERNS = ((128, 1), (512, 4), (2048, 16))
A_GROUPS = len(A_PATTERNS)
A_HEADS_PER_GROUP = 4
A_HEADS = A_GROUPS * A_HEADS_PER_GROUP
B_HEADS = 8
N_A = 3 * A_HEADS * HEAD_DIM
N_B = 3 * B_HEADS * HEAD_DIM
N_QKV = N_A + N_B
N_BUCKETS = 32
REL_MAX_DIST = 2048
CONV_WIDTH = 3
Q_BLOCK = 128
EPS = 1e-6
NEG = -1e30
SCALE = HEAD_DIM ** -0.5

LANES = 128
SUBLANES = 8
V7X_VMEM_BYTES = 64 * 1024 * 1024
VMEM_BUDGET_BYTES = V7X_VMEM_BYTES - 8 * 1024 * 1024

F32 = jnp.float32
BF16 = jnp.bfloat16


def _params(semantics, vmem_bytes):
    return pltpu.CompilerParams(
        dimension_semantics=semantics,
        vmem_limit_bytes=int(min(max(vmem_bytes, 16 * 1024 * 1024), VMEM_BUDGET_BYTES)),
    )


def _pick(n, candidates):
    for c in candidates:
        if c <= n and n % c == 0:
            return c
    return n


def _rms(x, g):
    return x * lax.rsqrt(jnp.mean(x * x, axis=-1, keepdims=True) + EPS) * g


def _norm_kernel(x_ref, g_ref, h_ref):
    h_ref[...] = _rms(x_ref[...], g_ref[...]).astype(h_ref.dtype)


def _pre_norm(x2, g):
    m, d = x2.shape
    bm = _pick(m, (512, 256, 128))
    return pl.pallas_call(
        _norm_kernel,
        grid=(m // bm,),
        in_specs=[pl.BlockSpec((bm, d), lambda i: (i, 0)), pl.BlockSpec((1, d), lambda i: (0, 0))],
        out_specs=pl.BlockSpec((bm, d), lambda i: (i, 0)),
        out_shape=jax.ShapeDtypeStruct((m, d), BF16),
        compiler_params=_params(("arbitrary",), 6 * bm * d * 4),
        name="pre_norm",
    )(x2, g.reshape(1, d))


def _mm_kernel(a_ref, w_ref, o_ref):
    o_ref[...] = jnp.dot(a_ref[...], w_ref[...], preferred_element_type=F32).astype(o_ref.dtype)


def _matmul(a, w, name):
    m, k = a.shape
    n = w.shape[1]
    bm = _pick(m, (1024, 512, 256, 128))
    bn = _pick(n, (1536, 1024, 512, 256, 128))
    vmem = 2 * 2 * (bm * k + k * bn + bm * bn) + 4 * bm * bn
    return pl.pallas_call(
        _mm_kernel,
        grid=(n // bn, m // bm),
        in_specs=[pl.BlockSpec((bm, k), lambda j, i: (i, 0)), pl.BlockSpec((k, bn), lambda j, i: (0, j))],
        out_specs=pl.BlockSpec((bm, bn), lambda j, i: (i, j)),
        out_shape=jax.ShapeDtypeStruct((m, n), BF16),
        compiler_params=_params(("arbitrary", "arbitrary"), vmem + vmem // 4),
        name=name,
    )(a, w)


def _lane_cumsum(x):
    n = x.shape[1]
    lane = lax.broadcasted_iota(jnp.int32, x.shape, 1)
    s = 1
    while s < n:
        x = x + jnp.where(lane >= s, pltpu.roll(x, s, 1), 0.0)
        s *= 2
    return x


def _forget_kernel(h_ref, wft_ref, bf_ref, crow_ref, ccol_ref):
    ft = lax.dot_general(wft_ref[...], h_ref[0], (((1,), (1,)), ((), ())), preferred_element_type=F32)
    z = ft + bf_ref[...]
    logf = jnp.minimum(z, 0.0) - jnp.log1p(jnp.exp(-jnp.abs(z)))
    c = _lane_cumsum(logf)
    crow_ref[0] = c[:B_HEADS]
    ccol_ref[0] = c.T


def _forget(h3, wft, bf_col):
    b, t, d = h3.shape
    return pl.pallas_call(
        _forget_kernel,
        grid=(b,),
        in_specs=[
            pl.BlockSpec((1, t, d), lambda i: (i, 0, 0)),
            pl.BlockSpec((LANES, d), lambda i: (0, 0)),
            pl.BlockSpec((LANES, 1), lambda i: (0, 0)),
        ],
        out_specs=[
            pl.BlockSpec((1, B_HEADS, t), lambda i: (i, 0, 0)),
            pl.BlockSpec((1, t, LANES), lambda i: (i, 0, 0)),
        ],
        out_shape=[
            jax.ShapeDtypeStruct((b, B_HEADS, t), F32),
            jax.ShapeDtypeStruct((b, t, LANES), F32),
        ],
        compiler_params=_params(("arbitrary",), 2 * 2 * t * d + 16 * t * LANES * 4),
        name="forget_cumsum",
    )(h3, wft, bf_col)


def _bias_kernel(tbl_ref, o_ref):
    hd = pl.program_id(0)
    grp = hd // A_HEADS_PER_GROUP
    dil = jnp.where(grp == 0, A_PATTERNS[0][1], jnp.where(grp == 1, A_PATTERNS[1][1], A_PATTERNS[2][1]))
    shape = (Q_BLOCK, 2 * Q_BLOCK)
    qi = lax.broadcasted_iota(jnp.int32, shape, 0)
    kj = lax.broadcasted_iota(jnp.int32, shape, 1)
    delta = qi + Q_BLOCK - kj
    dist = jnp.maximum(delta, 0) * dil
    max_exact = N_BUCKETS // 2
    nf = jnp.maximum(dist, 1).astype(F32)
    large = max_exact + (jnp.log(nf / max_exact) / math.log(REL_MAX_DIST / max_exact)
                         * (N_BUCKETS - max_exact)).astype(jnp.int32)
    large = jnp.minimum(large, N_BUCKETS - 1)
    bucket = jnp.where(dist < max_exact, dist, large)
    bias = jnp.zeros(shape, F32)
    for i in range(N_BUCKETS):
        bias = jnp.where(bucket == i, tbl_ref[i, hd], bias)
    span = Q_BLOCK
    valid = (delta >= 0) & (delta <= span)
    o_ref[0] = jnp.where(valid, bias, NEG)


def _bias_tiles(rel_bias):
    return pl.pallas_call(
        _bias_kernel,
        grid=(A_HEADS,),
        in_specs=[pl.BlockSpec(memory_space=pltpu.SMEM)],
        out_specs=pl.BlockSpec((1, Q_BLOCK, 2 * Q_BLOCK), lambda i: (i, 0, 0)),
        out_shape=jax.ShapeDtypeStruct((A_HEADS, Q_BLOCK, 2 * Q_BLOCK), F32),
        compiler_params=_params(("arbitrary",), 0),
        name="rel_bias_tiles",
    )(rel_bias)


def _dilated_kernel(*refs, t):
    qkv_refs = refs[:9]
    bias_refs = refs[9:12]
    o_ref = refs[12]
    stage, qd, kd, vd = refs[13:17]
    acc_s = refs[17:20]
    m_s = refs[20:23]
    l_s = refs[23:26]
    nblk = t // Q_BLOCK

    for g, (window, dil) in enumerate(A_PATTERNS):
        assert window // dil == Q_BLOCK
        q_ref, k_ref, v_ref = qkv_refs[3 * g:3 * g + 3]
        length = t // dil
        nb = length // Q_BLOCK
        if dil > 1:
            for src, dst in ((q_ref, qd), (k_ref, kd), (v_ref, vd)):
                stage[...] = src[0].astype(F32)
                for r in range(dil):
                    dst[r * length:(r + 1) * length, :] = stage[pl.ds(r, length, stride=dil), :].astype(BF16)
        bias = bias_refs[g][0]
        for blk in range(nblk):
            r, n = divmod(blk, nb)
            lo, hi = blk * Q_BLOCK, (blk + 1) * Q_BLOCK
            klo = lo if n == 0 else lo - Q_BLOCK
            if dil > 1:
                qn, kw, vw = qd[lo:hi, :], kd[klo:hi, :], vd[klo:hi, :]
            else:
                qn, kw, vw = q_ref[0, lo:hi, :], k_ref[0, klo:hi, :], v_ref[0, klo:hi, :]
            bt = bias[:, Q_BLOCK:] if n == 0 else bias
            s = lax.dot_general(qn, kw, (((1,), (1,)), ((), ())), preferred_element_type=F32) * SCALE + bt
            m = jnp.max(s, axis=-1, keepdims=True)
            p = jnp.exp(s - m)
            l = jnp.sum(p, axis=-1, keepdims=True)
            acc = jnp.dot(p.astype(BF16), vw, preferred_element_type=F32)
            if dil > 1:
                rows = pl.ds(n * Q_BLOCK * dil + r, Q_BLOCK, stride=dil)
            else:
                rows = pl.ds(lo, Q_BLOCK)
            acc_s[g][rows, :] = acc
            m_s[g][rows, :] = jnp.broadcast_to(m, (Q_BLOCK, HEAD_DIM))
            l_s[g][rows, :] = jnp.broadcast_to(l, (Q_BLOCK, HEAD_DIM))

    chunk = 256

    def merge(c, carry):
        rows = pl.ds(pl.multiple_of(c * chunk, chunk), chunk)
        ms = [m_s[g][rows, :] for g in range(A_GROUPS)]
        mx = jnp.maximum(jnp.maximum(ms[0], ms[1]), ms[2])
        num = jnp.zeros((chunk, HEAD_DIM), F32)
        den = jnp.zeros((chunk, HEAD_DIM), F32)
        for g in range(A_GROUPS):
            lg = l_s[g][rows, :]
            w = lg * jnp.exp(ms[g] - mx)
            num = num + w * (acc_s[g][rows, :] / lg)
            den = den + w
        o_ref[0, rows, :] = (num / den).astype(o_ref.dtype)
        return carry

    lax.fori_loop(0, t // chunk, merge, 0)


def _dilated(proj3, bias_tiles):
    b, t, _ = proj3.shape
    hpg = A_HEADS_PER_GROUP

    def col(kind, g):
        return lambda i, h: (i, 0, kind * A_HEADS + g * hpg + h)

    in_specs = []
    for g in range(A_GROUPS):
        for kind in range(3):
            in_specs.append(pl.BlockSpec((1, t, HEAD_DIM), col(kind, g)))
    for g in range(A_GROUPS):
        in_specs.append(pl.BlockSpec((1, Q_BLOCK, 2 * Q_BLOCK), lambda i, h, g=g: (g * hpg + h, 0, 0)))
    plane = t * HEAD_DIM
    scratch = [pltpu.VMEM((t, HEAD_DIM), F32)] + [pltpu.VMEM((t, HEAD_DIM), BF16)] * 3
    scratch += [pltpu.VMEM((t, HEAD_DIM), F32)] * 9
    vmem = 2 * 10 * plane * 2 + 10 * plane * 4 + 3 * plane * 2 + 16 * 1024 * 1024
    return pl.pallas_call(
        functools.partial(_dilated_kernel, t=t),
        grid=(b, hpg),
        in_specs=in_specs,
        out_specs=pl.BlockSpec((1, t, HEAD_DIM), lambda i, h: (i, 0, h)),
        out_shape=jax.ShapeDtypeStruct((b, t, hpg * HEAD_DIM), BF16),
        scratch_shapes=scratch,
        compiler_params=_params(("arbitrary", "arbitrary"), vmem),
        name="dilated_attention",
    )(*([proj3] * 9), *([bias_tiles] * 3))


def _fox_kernel(q_ref, k_ref, v_ref, crow_ref, ccol_ref, o_ref, *, tq):
    h = pl.program_id(1)
    qi = pl.program_id(2)
    q = q_ref[0]
    lane = lax.broadcasted_iota(jnp.int32, (tq, LANES), 1)
    c_t = jnp.sum(jnp.where(lane == h, ccol_ref[0], 0.0), axis=1, keepdims=True)

    def tile(j, carry, diagonal):
        m, l, acc = carry
        start = pl.multiple_of(j * tq, tq)
        k = k_ref[0, pl.ds(start, tq), :]
        v = v_ref[0, pl.ds(start, tq), :]
        c_s = crow_ref[0, pl.ds(h, 1), pl.ds(start, tq)]
        s = lax.dot_general(q, k, (((1,), (1,)), ((), ())), preferred_element_type=F32) * SCALE
        s = s + c_t - c_s
        if diagonal:
            row = lax.broadcasted_iota(jnp.int32, (tq, tq), 0)
            colv = lax.broadcasted_iota(jnp.int32, (tq, tq), 1)
            s = jnp.where(colv <= row, s, NEG)
        m_new = jnp.maximum(m, jnp.max(s, axis=-1, keepdims=True))
        alpha = jnp.exp(m - m_new)
        p = jnp.exp(s - m_new)
        l = alpha * l + jnp.sum(p, axis=-1, keepdims=True)
        acc = alpha * acc + jnp.dot(p.astype(BF16), v, preferred_element_type=F32)
        return m_new, l, acc

    init = (jnp.full((tq, 1), NEG, F32), jnp.zeros((tq, 1), F32), jnp.zeros((tq, HEAD_DIM), F32))
    carry = lax.fori_loop(0, qi, lambda j, c: tile(j, c, False), init)
    _, l, acc = tile(qi, carry, True)
    o_ref[0] = (acc / l).astype(o_ref.dtype)


def _fox(proj3, c_row, c_col):
    b, t, _ = proj3.shape
    tq = _pick(t, (256, 128))
    qoff = N_A // HEAD_DIM
    koff = qoff + B_HEADS
    voff = koff + B_HEADS
    plane = t * HEAD_DIM
    return pl.pallas_call(
        functools.partial(_fox_kernel, tq=tq),
        grid=(b, B_HEADS, t // tq),
        in_specs=[
            pl.BlockSpec((1, tq, HEAD_DIM), lambda i, h, q: (i, q, qoff + h)),
            pl.BlockSpec((1, t, HEAD_DIM), lambda i, h, q: (i, 0, koff + h)),
            pl.BlockSpec((1, t, HEAD_DIM), lambda i, h, q: (i, 0, voff + h)),
            pl.BlockSpec((1, B_HEADS, t), lambda i, h, q: (i, 0, 0)),
            pl.BlockSpec((1, tq, LANES), lambda i, h, q: (i, q, 0)),
        ],
        out_specs=pl.BlockSpec((1, tq, HEAD_DIM), lambda i, h, q: (i, q, h)),
        out_shape=jax.ShapeDtypeStruct((b, t, B_HEADS * HEAD_DIM), BF16),
        compiler_params=_params(("arbitrary", "arbitrary", "arbitrary"), 8 * plane * 2 + 8 * 1024 * 1024),
        name="fox_attention",
    )(proj3, proj3, proj3, c_row, c_col)


def _mix_kernel(ya_ref, yb_ref, ga_ref, gb_ref, wpa_ref, wpb_ref, wo_ref, x_ref, gpost_ref, gpre_ref,
                xo_ref, h_ref):
    pa = jnp.dot(ya_ref[...], wpa_ref[...], preferred_element_type=F32)
    pb = jnp.dot(yb_ref[...], wpb_ref[...], preferred_element_type=F32)
    ga = jax.nn.sigmoid(ga_ref[...].astype(F32))
    gb = jax.nn.sigmoid(gb_ref[...].astype(F32))
    merged = (ga * pa + gb * pb).astype(BF16)
    mo = jnp.dot(merged, wo_ref[...], preferred_element_type=F32)
    x_new = x_ref[...] + _rms(mo, gpost_ref[...])
    xo_ref[...] = x_new
    h_ref[...] = _rms(x_new, gpre_ref[...]).astype(h_ref.dtype)


def _mix(ya, yb, gates, wpa, wpb, wo, x2, g_post, g_pre):
    m, d = x2.shape
    ka, kb = ya.shape[1], yb.shape[1]
    bm = _pick(m, (256, 128))
    const = lambda i: (0, 0)
    single = dict(pipeline_mode=pl.Buffered(1))
    vmem = 2 * (ka + kb + d) * d + 2 * bm * (2 * (ka + kb + 2 * d) + 4 * d + 4 * d + 2 * d) + 8 * bm * d * 4
    return pl.pallas_call(
        _mix_kernel,
        grid=(m // bm,),
        in_specs=[
            pl.BlockSpec((bm, ka), lambda i: (i, 0)),
            pl.BlockSpec((bm, kb), lambda i: (i, 0)),
            pl.BlockSpec((bm, d), lambda i: (i, 0)),
            pl.BlockSpec((bm, d), lambda i: (i, 1)),
            pl.BlockSpec((ka, d), const, **single),
            pl.BlockSpec((kb, d), const, **single),
            pl.BlockSpec((d, d), const, **single),
            pl.BlockSpec((bm, d), lambda i: (i, 0)),
            pl.BlockSpec((1, d), const),
            pl.BlockSpec((1, d), const),
        ],
        out_specs=[pl.BlockSpec((bm, d), lambda i: (i, 0)), pl.BlockSpec((bm, d), lambda i: (i, 0))],
        out_shape=[jax.ShapeDtypeStruct((m, d), F32), jax.ShapeDtypeStruct((m, d), BF16)],
        compiler_params=_params(("arbitrary",), vmem),
        name="gated_mix",
    )(ya, yb, gates, gates, wpa, wpb, wo, x2, g_post.reshape(1, d), g_pre.reshape(1, d))


def _causal_conv3(u, carry_ref, cw_ref, cb_ref):
    rows = u.shape[0]
    ext = jnp.concatenate([carry_ref[...], u], axis=0)
    x1 = pltpu.roll(ext, 1, 0)[SUBLANES:]
    x2 = pltpu.roll(ext, 2, 0)[SUBLANES:]
    carry_ref[...] = u[rows - SUBLANES:]
    cw = cw_ref[...]
    return cw[0:1] * x2 + cw[1:2] * x1 + cw[2:3] * u + cb_ref[...]


def _ffn_up_kernel(h_ref, wg_ref, wv_ref, cwg_ref, cwv_ref, cbg_ref, cbv_ref, o_ref, carry_g, carry_v,
                   *, blocks_per_seq):
    i = pl.program_id(1)

    @pl.when(i % blocks_per_seq == 0)
    def _():
        carry_g[...] = jnp.zeros_like(carry_g)
        carry_v[...] = jnp.zeros_like(carry_v)

    h = h_ref[...]
    ug = jnp.dot(h, wg_ref[...], preferred_element_type=F32)
    uv = jnp.dot(h, wv_ref[...], preferred_element_type=F32)
    gate = _causal_conv3(ug, carry_g, cwg_ref, cbg_ref)
    val = _causal_conv3(uv, carry_v, cwv_ref, cbv_ref)
    cdf = 0.5 * (1.0 + jnp.tanh(math.sqrt(2.0 / math.pi) * (gate + 0.044715 * (gate * gate * gate))))
    o_ref[...] = (gate * cdf * val).astype(o_ref.dtype)


def _ffn_up(h2, w_up, conv_w, conv_b, t):
    m, d = h2.shape
    dff = w_up.shape[1] // 2
    bn = _pick(dff, (512, 256, 128))
    nj = dff // bn
    bm = _pick(t, (1024, 512, 256, 128))
    vmem = 2 * 2 * (bm * d + 2 * d * bn + bm * bn) + 10 * bm * bn * 4
    return pl.pallas_call(
        functools.partial(_ffn_up_kernel, blocks_per_seq=t // bm),
        grid=(nj, m // bm),
        in_specs=[
            pl.BlockSpec((bm, d), lambda j, i: (i, 0)),
            pl.BlockSpec((d, bn), lambda j, i: (0, j)),
            pl.BlockSpec((d, bn), lambda j, i: (0, nj + j)),
            pl.BlockSpec((CONV_WIDTH, bn), lambda j, i: (0, j)),
            pl.BlockSpec((CONV_WIDTH, bn), lambda j, i: (0, nj + j)),
            pl.BlockSpec((1, bn), lambda j, i: (0, j)),
            pl.BlockSpec((1, bn), lambda j, i: (0, nj + j)),
        ],
        out_specs=pl.BlockSpec((bm, bn), lambda j, i: (i, j)),
        out_shape=jax.ShapeDtypeStruct((m, dff), BF16),
        scratch_shapes=[pltpu.VMEM((SUBLANES, bn), F32), pltpu.VMEM((SUBLANES, bn), F32)],
        compiler_params=_params(("arbitrary", "arbitrary"), vmem),
        name="ffn_up_conv_act",
    )(h2, w_up, w_up, conv_w, conv_w, conv_b.reshape(1, -1), conv_b.reshape(1, -1))


def _ffn_down_kernel(a_ref, w_ref, x_ref, gpost_ref, gpre_ref, xo_ref, *h_ref):
    y = jnp.dot(a_ref[...], w_ref[...], preferred_element_type=F32)
    x_new = x_ref[...] + _rms(y, gpost_ref[...])
    xo_ref[...] = x_new
    if h_ref:
        h_ref[0][...] = _rms(x_new, gpre_ref[...]).astype(h_ref[0].dtype)


def _ffn_down(act, w_down, x2, g_post, g_pre_next):
    m, d = x2.shape
    dff = act.shape[1]
    bm = _pick(m, (256, 128))
    with_h = g_pre_next is not None
    g_pre = g_pre_next if with_h else g_post
    out_specs = [pl.BlockSpec((bm, d), lambda i: (i, 0))]
    out_shape = [jax.ShapeDtypeStruct((m, d), F32)]
    if with_h:
        out_specs.append(pl.BlockSpec((bm, d), lambda i: (i, 0)))
        out_shape.append(jax.ShapeDtypeStruct((m, d), BF16))
    vmem = 2 * dff * d + 2 * bm * (2 * dff + 4 * d + 4 * d + 2 * d) + 6 * bm * d * 4
    outs = pl.pallas_call(
        _ffn_down_kernel,
        grid=(m // bm,),
        in_specs=[
            pl.BlockSpec((bm, dff), lambda i: (i, 0)),
            pl.BlockSpec((dff, d), lambda i: (0, 0), pipeline_mode=pl.Buffered(1)),
            pl.BlockSpec((bm, d), lambda i: (i, 0)),
            pl.BlockSpec((1, d), lambda i: (0, 0)),
            pl.BlockSpec((1, d), lambda i: (0, 0)),
        ],
        out_specs=out_specs,
        out_shape=out_shape,
        compiler_params=_params(("arbitrary",), vmem),
        name="ffn_down_norm",
    )(act, w_down, x2, g_post.reshape(1, d), g_pre.reshape(1, d))
    return (outs[0], outs[1]) if with_h else (outs[0], None)


def kernel(x, rel_bias, w_in, b_f, w_pa, w_pb, w_o, w_up, conv_w, conv_b, w_down,
           g_mix_pre, g_mix_post, g_ffn_pre, g_ffn_post):
    b, t, d = x.shape
    depth = w_in.shape[0]
    m = b * t
    assert t % max(dil for _, dil in A_PATTERNS) == 0 and (t // A_PATTERNS[-1][1]) % Q_BLOCK == 0
    assert w_in.shape[2] == N_QKV + B_HEADS + 2 * d

    bias_tiles = _bias_tiles(rel_bias)
    x2 = x.reshape(m, d)
    h = _pre_norm(x2, g_mix_pre[0])
    for layer in range(depth):
        wl = w_in[layer]
        w_qkv = wl[:, :N_QKV].astype(BF16)
        w_f = wl[:, N_QKV:N_QKV + B_HEADS]
        w_gate = wl[:, N_QKV + B_HEADS:].astype(BF16)
        wft = jnp.zeros((LANES, d), BF16).at[:B_HEADS].set(w_f.T.astype(BF16))
        bf_col = jnp.zeros((LANES, 1), F32).at[:B_HEADS, 0].set(b_f[layer])

        proj = _matmul(h, w_qkv, "proj_qkv")
        gates = _matmul(h, w_gate, "proj_gates")
        c_row, c_col = _forget(h.reshape(b, t, d), wft, bf_col)
        proj3 = proj.reshape(b, t, N_QKV)
        y_a = _dilated(proj3, bias_tiles)
        y_b = _fox(proj3, c_row, c_col)
        x2, h2 = _mix(y_a.reshape(m, -1), y_b.reshape(m, -1), gates,
                      w_pa[layer].astype(BF16), w_pb[layer].astype(BF16), w_o[layer].astype(BF16),
                      x2, g_mix_post[layer], g_ffn_pre[layer])
        act = _ffn_up(h2, w_up[layer].astype(BF16), conv_w[layer], conv_b[layer], t)
        g_next = g_mix_pre[layer + 1] if layer + 1 < depth else None
        x2, h = _ffn_down(act, w_down[layer].astype(BF16), x2, g_ffn_post[layer], g_next)
    return x2.reshape(b, t, d)
```

```python
import functools
import math

import jax
import jax.numpy as jnp
from jax import lax
from jax.experimental import pallas as pl
from jax.experimental.pallas import tpu as pltpu

HEAD_DIM = 128
A_PATTERNS = ((128, 1), (512, 4), (2048, 16))
A_GROUPS = len(A_PATTERNS)
A_HEADS_PER_GROUP = 4
A_HEADS = A_GROUPS * A_HEADS_PER_GROUP
B_HEADS = 8
N_A = 3 * A_HEADS * HEAD_DIM
N_B = 3 * B_HEADS * HEAD_DIM
N_QKV = N_A + N_B
N_BUCKETS = 32
REL_MAX_DIST = 2048
CONV_WIDTH = 3
Q_BLOCK = 128
EPS = 1e-6
NEG = -1e30
SCALE = HEAD_DIM ** -0.5

LANES = 128
SUBLANES = 8
V7X_VMEM_BYTES = 64 * 1024 * 1024
VMEM_BUDGET_BYTES = V7X_VMEM_BYTES - 8 * 1024 * 1024

F32 = jnp.float32
BF16 = jnp.bfloat16


def _params(semantics, vmem_bytes):
    return pltpu.CompilerParams(
        dimension_semantics=semantics,
        vmem_limit_bytes=int(min(max(vmem_bytes, 16 * 1024 * 1024), VMEM_BUDGET_BYTES)),
    )


def _pick(n, candidates):
    for c in candidates:
        if c <= n and n % c == 0:
            return c
    return n


def _rms(x, g):
    return x * lax.rsqrt(jnp.mean(x * x, axis=-1, keepdims=True) + EPS) * g


def _norm_kernel(x_ref, g_ref, h_ref):
    h_ref[...] = _rms(x_ref[...], g_ref[...]).astype(h_ref.dtype)


def _pre_norm(x2, g):
    m, d = x2.shape
    bm = _pick(m, (512, 256, 128))
    return pl.pallas_call(
        _norm_kernel,
        grid=(m // bm,),
        in_specs=[pl.BlockSpec((bm, d), lambda i: (i, 0)), pl.BlockSpec((1, d), lambda i: (0, 0))],
        out_specs=pl.BlockSpec((bm, d), lambda i: (i, 0)),
        out_shape=jax.ShapeDtypeStruct((m, d), BF16),
        compiler_params=_params(("arbitrary",), 6 * bm * d * 4),
        name="pre_norm",
    )(x2, g.reshape(1, d))


def _mm_kernel(a_ref, w_ref, o_ref):
    o_ref[...] = jnp.dot(a_ref[...], w_ref[...], preferred_element_type=F32).astype(o_ref.dtype)


def _mm_cast_kernel(a_ref, w_ref, o_ref):
    w = w_ref[...].astype(BF16)
    o_ref[...] = jnp.dot(a_ref[...], w, preferred_element_type=F32).astype(o_ref.dtype)


def _matmul(a, w, name):
    m, k = a.shape
    n = w.shape[1]
    bm = _pick(m, (1024, 512, 256, 128))
    bn = _pick(n, (1536, 1024, 512, 256, 128))
    vmem = 2 * 2 * (bm * k + k * bn + bm * bn) + 4 * bm * bn
    return pl.pallas_call(
        _mm_kernel,
        grid=(n // bn, m // bm),
        in_specs=[pl.BlockSpec((bm, k), lambda j, i: (i, 0)), pl.BlockSpec((k, bn), lambda j, i: (0, j))],
        out_specs=pl.BlockSpec((bm, bn), lambda j, i: (i, j)),
        out_shape=jax.ShapeDtypeStruct((m, n), BF16),
        compiler_params=_params(("arbitrary", "arbitrary"), vmem + vmem // 4),
        name=name,
    )(a, w)


def _matmul_f32w(a, w_all, layer, n, name):
    m, k = a.shape
    bm = _pick(m, (1024, 512, 256, 128))
    bn = _pick(n, (1280, 1024, 768, 512, 256, 128))
    vmem = 2 * (2 * bm * k + 4 * k * bn + 2 * bm * bn) + 2 * k * bn + 2 * 4 * bm * bn
    return pl.pallas_call(
        _mm_cast_kernel,
        grid=(n // bn, m // bm),
        in_specs=[pl.BlockSpec((bm, k), lambda j, i: (i, 0)),
                  pl.BlockSpec((None, k, bn), lambda j, i: (layer, 0, j))],
        out_specs=pl.BlockSpec((bm, bn), lambda j, i: (i, j)),
        out_shape=jax.ShapeDtypeStruct((m, n), BF16),
        compiler_params=_params(("arbitrary", "arbitrary"), vmem),
        name=name,
    )(a, w_all)


def _lane_cumsum(x):
    n = x.shape[1]
    lane = lax.broadcasted_iota(jnp.int32, x.shape, 1)
    s = 1
    while s < n:
        x = x + jnp.where(lane >= s, pltpu.roll(x, s, 1), 0.0)
        s *= 2
    return x


def _forget_kernel(h_ref, wft_ref, bf_ref, crow_ref, ccol_ref):
    ft = lax.dot_general(wft_ref[...], h_ref[0], (((1,), (1,)), ((), ())), preferred_element_type=F32)
    z = ft + bf_ref[...]
    logf = jnp.minimum(z, 0.0) - jnp.log1p(jnp.exp(-jnp.abs(z)))
    c = _lane_cumsum(logf)
    crow_ref[0] = c[:B_HEADS]
    ccol_ref[0] = c.T


def _forget(h3, wft, bf_col):
    b, t, d = h3.shape
    return pl.pallas_call(
        _forget_kernel,
        grid=(b,),
        in_specs=[
            pl.BlockSpec((1, t, d), lambda i: (i, 0, 0)),
            pl.BlockSpec((LANES, d), lambda i: (0, 0)),
            pl.BlockSpec((LANES, 1), lambda i: (0, 0)),
        ],
        out_specs=[
            pl.BlockSpec((1, B_HEADS, t), lambda i: (i, 0, 0)),
            pl.BlockSpec((1, t, LANES), lambda i: (i, 0, 0)),
        ],
        out_shape=[
            jax.ShapeDtypeStruct((b, B_HEADS, t), F32),
            jax.ShapeDtypeStruct((b, t, LANES), F32),
        ],
        compiler_params=_params(("arbitrary",), 2 * 2 * t * d + 16 * t * LANES * 4),
        name="forget_cumsum",
    )(h3, wft, bf_col)


def _bias_kernel(tbl_ref, o_ref):
    hd = pl.program_id(0)
    grp = hd // A_HEADS_PER_GROUP
    dil = jnp.where(grp == 0, A_PATTERNS[0][1], jnp.where(grp == 1, A_PATTERNS[1][1], A_PATTERNS[2][1]))
    shape = (Q_BLOCK, 2 * Q_BLOCK)
    qi = lax.broadcasted_iota(jnp.int32, shape, 0)
    kj = lax.broadcasted_iota(jnp.int32, shape, 1)
    delta = qi + Q_BLOCK - kj
    dist = jnp.maximum(delta, 0) * dil
    max_exact = N_BUCKETS // 2
    nf = jnp.maximum(dist, 1).astype(F32)
    large = max_exact + (jnp.log(nf / max_exact) / math.log(REL_MAX_DIST / max_exact)
                         * (N_BUCKETS - max_exact)).astype(jnp.int32)
    large = jnp.minimum(large, N_BUCKETS - 1)
    bucket = jnp.where(dist < max_exact, dist, large)
    bias = jnp.zeros(shape, F32)
    for i in range(N_BUCKETS):
        bias = jnp.where(bucket == i, tbl_ref[i, hd], bias)
    span = Q_BLOCK
    valid = (delta >= 0) & (delta <= span)
    o_ref[0] = jnp.where(valid, bias, NEG)


def _bias_tiles(rel_bias):
    return pl.pallas_call(
        _bias_kernel,
        grid=(A_HEADS,),
        in_specs=[pl.BlockSpec(memory_space=pltpu.SMEM)],
        out_specs=pl.BlockSpec((1, Q_BLOCK, 2 * Q_BLOCK), lambda i: (i, 0, 0)),
        out_shape=jax.ShapeDtypeStruct((A_HEADS, Q_BLOCK, 2 * Q_BLOCK), F32),
        compiler_params=_params(("arbitrary",), 0),
        name="rel_bias_tiles",
    )(rel_bias)


def _dilated_kernel(*refs, t):
    qkv_refs = refs[:9]
    bias_refs = refs[9:12]
    o_ref = refs[12]
    stage, qd, kd, vd = refs[13:17]
    acc_s = refs[17:20]
    m_s = refs[20:23]
    l_s = refs[23:26]
    nblk = t // Q_BLOCK

    for g, (window, dil) in enumerate(A_PATTERNS):
        assert window // dil == Q_BLOCK
        q_ref, k_ref, v_ref = qkv_refs[3 * g:3 * g + 3]
        length = t // dil
        nb = length // Q_BLOCK
        if dil > 1:
            for src, dst in ((q_ref, qd), (k_ref, kd), (v_ref, vd)):
                stage[...] = src[0].astype(F32)
                for r in range(dil):
                    dst[r * length:(r + 1) * length, :] = stage[pl.ds(r, length, stride=dil), :].astype(BF16)
        bias = bias_refs[g][0]
        for blk in range(nblk):
            r, n = divmod(blk, nb)
            lo, hi = blk * Q_BLOCK, (blk + 1) * Q_BLOCK
            klo = lo if n == 0 else lo - Q_BLOCK
            if dil > 1:
                qn, kw, vw = qd[lo:hi, :], kd[klo:hi, :], vd[klo:hi, :]
            else:
                qn, kw, vw = q_ref[0, lo:hi, :], k_ref[0, klo:hi, :], v_ref[0, klo:hi, :]
            bt = bias[:, Q_BLOCK:] if n == 0 else bias
            s = lax.dot_general(qn, kw, (((1,), (1,)), ((), ())), preferred_element_type=F32) * SCALE + bt
            m = jnp.max(s, axis=-1, keepdims=True)
            p = jnp.exp(s - m)
            l = jnp.sum(p, axis=-1, keepdims=True)
            acc = jnp.dot(p.astype(BF16), vw, preferred_element_type=F32)
            if dil > 1:
                rows = pl.ds(n * Q_BLOCK * dil + r, Q_BLOCK, stride=dil)
            else:
                rows = pl.ds(lo, Q_BLOCK)
            acc_s[g][rows, :] = acc
            m_s[g][rows, :] = jnp.broadcast_to(m, (Q_BLOCK, HEAD_DIM))
            l_s[g][rows, :] = jnp.broadcast_to(l, (Q_BLOCK, HEAD_DIM))

    chunk = 256

    def merge(c, carry):
        rows = pl.ds(pl.multiple_of(c * chunk, chunk), chunk)
        ms = [m_s[g][rows, :] for g in range(A_GROUPS)]
        mx = jnp.maximum(jnp.maximum(ms[0], ms[1]), ms[2])
        num = jnp.zeros((chunk, HEAD_DIM), F32)
        den = jnp.zeros((chunk, HEAD_DIM), F32)
        for g in range(A_GROUPS):
            lg = l_s[g][rows, :]
            w = lg * jnp.exp(ms[g] - mx)
            num = num + w * (acc_s[g][rows, :] / lg)
            den = den + w
        o_ref[0, rows, :] = (num / den).astype(o_ref.dtype)
        return carry

    lax.fori_loop(0, t // chunk, merge, 0)


def _dilated(proj3, bias_tiles):
    b, t, _ = proj3.shape
    hpg = A_HEADS_PER_GROUP

    def col(kind, g):
        return lambda i, h: (i, 0, kind * A_HEADS + g * hpg + h)

    in_specs = []
    for g in range(A_GROUPS):
        for kind in range(3):
            in_specs.append(pl.BlockSpec((1, t, HEAD_DIM), col(kind, g)))
    for g in range(A_GROUPS):
        in_specs.append(pl.BlockSpec((1, Q_BLOCK, 2 * Q_BLOCK), lambda i, h, g=g: (g * hpg + h, 0, 0)))
    plane = t * HEAD_DIM
    scratch = [pltpu.VMEM((t, HEAD_DIM), F32)] + [pltpu.VMEM((t, HEAD_DIM), BF16)] * 3
    scratch += [pltpu.VMEM((t, HEAD_DIM), F32)] * 9
    vmem = 2 * 10 * plane * 2 + 10 * plane * 4 + 3 * plane * 2 + 16 * 1024 * 1024
    return pl.pallas_call(
        functools.partial(_dilated_kernel, t=t),
        grid=(b, hpg),
        in_specs=in_specs,
        out_specs=pl.BlockSpec((1, t, HEAD_DIM), lambda i, h: (i, 0, h)),
        out_shape=jax.ShapeDtypeStruct((b, t, hpg * HEAD_DIM), BF16),
        scratch_shapes=scratch,
        compiler_params=_params(("arbitrary", "arbitrary"), vmem),
        name="dilated_attention",
    )(*([proj3] * 9), *([bias_tiles] * 3))


LOG2E = math.log2(math.e)
FOX_HEADS_PER_STEP = 4
FOX_TQ = 256
FOX_TK = 512


def _fox_kernel(q_ref, k_ref, v_ref, crow_ref, ccol_ref, o_ref, vaug_ref, *, tq, tk, nh):
    hb = pl.program_id(1)
    qi = pl.program_id(2)
    wide = 2 * HEAD_DIM

    @pl.when(qi == 0)
    def _():
        for i in range(nh):
            vaug_ref[:, i * wide:i * wide + HEAD_DIM] = v_ref[0, :, i * HEAD_DIM:(i + 1) * HEAD_DIM]
            vaug_ref[:, i * wide + HEAD_DIM:(i + 1) * wide] = jnp.ones((vaug_ref.shape[0], HEAD_DIM), BF16)

    lane = lax.broadcasted_iota(jnp.int32, (tq, LANES), 1)
    ccol = ccol_ref[0]
    qs, cts = [], []
    for i in range(nh):
        qs.append(q_ref[0, :, i * HEAD_DIM:(i + 1) * HEAD_DIM])
        c_t = jnp.sum(jnp.where(lane == hb * nh + i, ccol, 0.0), axis=1, keepdims=True)
        cts.append(c_t * LOG2E)

    def tile(j, carry, diagonal):
        start = pl.multiple_of(j * tk, tk)
        out = []
        for i in range(nh):
            m, acc = carry[i]
            k = k_ref[0, pl.ds(start, tk), i * HEAD_DIM:(i + 1) * HEAD_DIM]
            va = vaug_ref[pl.ds(start, tk), i * wide:(i + 1) * wide]
            c_s = crow_ref[0, pl.ds(hb * nh + i, 1), pl.ds(start, tk)] * LOG2E
            z = lax.dot_general(qs[i], k, (((1,), (1,)), ((), ())), preferred_element_type=F32)
            a = z * (SCALE * LOG2E) - c_s
            if diagonal:
                row = lax.broadcasted_iota(jnp.int32, (tq, tk), 0)
                colv = lax.broadcasted_iota(jnp.int32, (tq, tk), 1)
                a = jnp.where(colv - row <= qi * tq - j * tk, a, NEG)
            m_new = jnp.maximum(m, jnp.max(a, axis=-1, keepdims=True) + cts[i])
            alpha = jnp.exp2(m - m_new)
            p = jnp.exp2(a + (cts[i] - m_new))
            acc = alpha * acc + jnp.dot(p.astype(BF16), va, preferred_element_type=F32)
            out.append((m_new, acc))
        return tuple(out)

    init = tuple((jnp.full((tq, 1), NEG, F32), jnp.zeros((tq, wide), F32)) for _ in range(nh))
    n_full = (qi * tq) // tk
    carry = lax.fori_loop(0, n_full, lambda j, c: tile(j, c, False), init)
    for r in range(max(tq // tk, 1)):
        carry = tile(n_full + r, carry, True)
    for i in range(nh):
        _, acc = carry[i]
        o_ref[0, :, i * HEAD_DIM:(i + 1) * HEAD_DIM] = (acc[:, :HEAD_DIM] / acc[:, HEAD_DIM:]).astype(o_ref.dtype)


def _fox(proj3, c_row, c_col):
    b, t, _ = proj3.shape
    nh = FOX_HEADS_PER_STEP
    tq = _pick(t, (FOX_TQ, 128))
    tk = _pick(t, (FOX_TK, 128))
    assert tq % tk == 0 or tk % tq == 0
    width = nh * HEAD_DIM
    qoff = N_A // width
    koff = qoff + B_HEADS // nh
    voff = koff + B_HEADS // nh
    assert N_A % width == 0 and B_HEADS % nh == 0
    return pl.pallas_call(
        functools.partial(_fox_kernel, tq=tq, tk=tk, nh=nh),
        grid=(b, B_HEADS // nh, t // tq),
        in_specs=[
            pl.BlockSpec((1, tq, width), lambda i, h, q: (i, q, qoff + h)),
            pl.BlockSpec((1, t, width), lambda i, h, q: (i, 0, koff + h)),
            pl.BlockSpec((1, t, width), lambda i, h, q: (i, 0, voff + h)),
            pl.BlockSpec((1, B_HEADS, t), lambda i, h, q: (i, 0, 0)),
            pl.BlockSpec((1, tq, LANES), lambda i, h, q: (i, q, 0)),
        ],
        out_specs=pl.BlockSpec((1, tq, width), lambda i, h, q: (i, q, h)),
        out_shape=jax.ShapeDtypeStruct((b, t, B_HEADS * HEAD_DIM), BF16),
        scratch_shapes=[pltpu.VMEM((t, 2 * width), BF16)],
        compiler_params=_params(("arbitrary", "arbitrary", "arbitrary"), 8 * t * width * 2 + 16 * 1024 * 1024),
        name="fox_attention",
    )(proj3, proj3, proj3, c_row, c_col)


def _mix_kernel(ya_ref, yb_ref, ga_ref, gb_ref, wpa_ref, wpb_ref, wo_ref, x_ref, gpost_ref, gpre_ref,
                xo_ref, h_ref):
    pa = jnp.dot(ya_ref[...], wpa_ref[...], preferred_element_type=F32)
    pb = jnp.dot(yb_ref[...], wpb_ref[...], preferred_element_type=F32)
    ga = jax.nn.sigmoid(ga_ref[...].astype(F32))
    gb = jax.nn.sigmoid(gb_ref[...].astype(F32))
    merged = (ga * pa + gb * pb).astype(BF16)
    mo = jnp.dot(merged, wo_ref[...], preferred_element_type=F32)
    x_new = x_ref[...] + _rms(mo, gpost_ref[...])
    xo_ref[...] = x_new
    h_ref[...] = _rms(x_new, gpre_ref[...]).astype(h_ref.dtype)


def _mix(ya, yb, gates, wpa, wpb, wo, x2, g_post, g_pre):
    m, d = x2.shape
    ka, kb = ya.shape[1], yb.shape[1]
    bm = _pick(m, (256, 128))
    const = lambda i: (0, 0)
    single = dict(pipeline_mode=pl.Buffered(1))
    vmem = 2 * (ka + kb + d) * d + 2 * bm * (2 * (ka + kb + 2 * d) + 4 * d + 4 * d + 2 * d) + 8 * bm * d * 4
    return pl.pallas_call(
        _mix_kernel,
        grid=(m // bm,),
        in_specs=[
            pl.BlockSpec((bm, ka), lambda i: (i, 0)),
            pl.BlockSpec((bm, kb), lambda i: (i, 0)),
            pl.BlockSpec((bm, d), lambda i: (i, 0)),
            pl.BlockSpec((bm, d), lambda i: (i, 1)),
            pl.BlockSpec((ka, d), const, **single),
            pl.BlockSpec((kb, d), const, **single),
            pl.BlockSpec((d, d), const, **single),
            pl.BlockSpec((bm, d), lambda i: (i, 0)),
            pl.BlockSpec((1, d), const),
            pl.BlockSpec((1, d), const),
        ],
        out_specs=[pl.BlockSpec((bm, d), lambda i: (i, 0)), pl.BlockSpec((bm, d), lambda i: (i, 0))],
        out_shape=[jax.ShapeDtypeStruct((m, d), F32), jax.ShapeDtypeStruct((m, d), BF16)],
        compiler_params=_params(("arbitrary",), vmem),
        name="gated_mix",
    )(ya, yb, gates, gates, wpa, wpb, wo, x2, g_post.reshape(1, d), g_pre.reshape(1, d))


def _ffn_up_kernel(h_ref, wg_ref, wv_ref, cwg_ref, cwv_ref, cbg_ref, cbv_ref, o_ref, carry_g, carry_v,
                   *, blocks_per_seq):
    i = pl.program_id(1)
    bm = h_ref.shape[0]

    @pl.when(i % blocks_per_seq == 0)
    def _():
        carry_g[...] = jnp.zeros_like(carry_g)
        carry_v[...] = jnp.zeros_like(carry_v)

    def conv3(u, carry_ref, cw, cb):
        ext = jnp.concatenate([carry_ref[...], u], axis=0)
        x1 = pltpu.roll(ext, 1, 0)[SUBLANES:]
        x2 = pltpu.roll(ext, 2, 0)[SUBLANES:]
        carry_ref[...] = u[bm - SUBLANES:]
        return cw[0:1] * x2 + cw[1:2] * x1 + cw[2:3] * u + cb

    h = h_ref[...]
    ug = jnp.dot(h, wg_ref[...].astype(BF16), preferred_element_type=F32)
    uv = jnp.dot(h, wv_ref[...].astype(BF16), preferred_element_type=F32)
    gate = conv3(ug, carry_g, cwg_ref[...], cbg_ref[...])
    val = conv3(uv, carry_v, cwv_ref[...], cbv_ref[...])
    cdf = 0.5 * (1.0 + jnp.tanh(math.sqrt(2.0 / math.pi) * (gate + 0.044715 * (gate * gate * gate))))
    o_ref[...] = (gate * cdf * val).astype(o_ref.dtype)


def _ffn_up(h2, w_up_all, layer, conv_w, conv_b, t):
    m, d = h2.shape
    dff = w_up_all.shape[2] // 2
    bn = _pick(dff, (512, 256, 128))
    nj = dff // bn
    bm = _pick(t, (1024, 512, 256, 128))
    vmem = 2 * (2 * bm * d + 2 * 4 * d * bn + 2 * bm * bn) + 2 * 2 * d * bn + 32 * bm * bn
    return pl.pallas_call(
        functools.partial(_ffn_up_kernel, blocks_per_seq=t // bm),
        grid=(nj, m // bm),
        in_specs=[
            pl.BlockSpec((bm, d), lambda j, i: (i, 0)),
            pl.BlockSpec((None, d, bn), lambda j, i: (layer, 0, j)),
            pl.BlockSpec((None, d, bn), lambda j, i: (layer, 0, nj + j)),
            pl.BlockSpec((CONV_WIDTH, bn), lambda j, i: (0, j)),
            pl.BlockSpec((CONV_WIDTH, bn), lambda j, i: (0, nj + j)),
            pl.BlockSpec((1, bn), lambda j, i: (0, j)),
            pl.BlockSpec((1, bn), lambda j, i: (0, nj + j)),
        ],
        out_specs=pl.BlockSpec((bm, bn), lambda j, i: (i, j)),
        out_shape=jax.ShapeDtypeStruct((m, dff), BF16),
        scratch_shapes=[pltpu.VMEM((SUBLANES, bn), F32), pltpu.VMEM((SUBLANES, bn), F32)],
        compiler_params=_params(("arbitrary", "arbitrary"), vmem),
        name="ffn_up_conv_act",
    )(h2, w_up_all, w_up_all, conv_w, conv_w, conv_b.reshape(1, -1), conv_b.reshape(1, -1))


def _ffn_down_kernel(a_ref, w_ref, x_ref, gpost_ref, gpre_ref, xo_ref, *h_ref):
    y = jnp.dot(a_ref[...], w_ref[...], preferred_element_type=F32)
    x_new = x_ref[...] + _rms(y, gpost_ref[...])
    xo_ref[...] = x_new
    if h_ref:
        h_ref[0][...] = _rms(x_new, gpre_ref[...]).astype(h_ref[0].dtype)


def _ffn_down(act, w_down, x2, g_post, g_pre_next):
    m, d = x2.shape
    dff = act.shape[1]
    bm = _pick(m, (256, 128))
    with_h = g_pre_next is not None
    g_pre = g_pre_next if with_h else g_post
    out_specs = [pl.BlockSpec((bm, d), lambda i: (i, 0))]
    out_shape = [jax.ShapeDtypeStruct((m, d), F32)]
    if with_h:
        out_specs.append(pl.BlockSpec((bm, d), lambda i: (i, 0)))
        out_shape.append(jax.ShapeDtypeStruct((m, d), BF16))
    vmem = 2 * dff * d + 2 * bm * (2 * dff + 4 * d + 4 * d + 2 * d) + 6 * bm * d * 4
    outs = pl.pallas_call(
        _ffn_down_kernel,
        grid=(m // bm,),
        in_specs=[
            pl.BlockSpec((bm, dff), lambda i: (i, 0)),
            pl.BlockSpec((dff, d), lambda i: (0, 0), pipeline_mode=pl.Buffered(1)),
            pl.BlockSpec((bm, d), lambda i: (i, 0)),
            pl.BlockSpec((1, d), lambda i: (0, 0)),
            pl.BlockSpec((1, d), lambda i: (0, 0)),
        ],
        out_specs=out_specs,
        out_shape=out_shape,
        compiler_params=_params(("arbitrary",), vmem),
        name="ffn_down_norm",
    )(act, w_down, x2, g_post.reshape(1, d), g_pre.reshape(1, d))
    return (outs[0], outs[1]) if with_h else (outs[0], None)


def kernel(x, rel_bias, w_in, b_f, w_pa, w_pb, w_o, w_up, conv_w, conv_b, w_down,
           g_mix_pre, g_mix_post, g_ffn_pre, g_ffn_post):
    b, t, d = x.shape
    depth = w_in.shape[0]
    m = b * t
    assert t % max(dil for _, dil in A_PATTERNS) == 0 and (t // A_PATTERNS[-1][1]) % Q_BLOCK == 0
    assert w_in.shape[2] == N_QKV + B_HEADS + 2 * d

    bias_tiles = _bias_tiles(rel_bias)
    x2 = x.reshape(m, d)
    h = _pre_norm(x2, g_mix_pre[0])
    for layer in range(depth):
        wl = w_in[layer]
        w_f = wl[:, N_QKV:N_QKV + B_HEADS]
        w_gate = wl[:, N_QKV + B_HEADS:].astype(BF16)
        wft = jnp.zeros((LANES, d), BF16).at[:B_HEADS].set(w_f.T.astype(BF16))
        bf_col = jnp.zeros((LANES, 1), F32).at[:B_HEADS, 0].set(b_f[layer])

        proj = _matmul_f32w(h, w_in, layer, N_QKV, "proj_qkv")
        gates = _matmul(h, w_gate, "proj_gates")
        c_row, c_col = _forget(h.reshape(b, t, d), wft, bf_col)
        proj3 = proj.reshape(b, t, N_QKV)
        y_a = _dilated(proj3, bias_tiles)
        y_b = _fox(proj3, c_row, c_col)
        x2, h2 = _mix(y_a.reshape(m, -1), y_b.reshape(m, -1), gates,
                      w_pa[layer].astype(BF16), w_pb[layer].astype(BF16), w_o[layer].astype(BF16),
                      x2, g_mix_post[layer], g_ffn_pre[layer])
        act = _ffn_up(h2, w_up, layer, conv_w[layer], conv_b[layer], t)
        g_next = g_mix_pre[layer + 1] if layer + 1 < depth else None
        x2, h = _ffn_down(act, w_down[layer].astype(BF16), x2, g_ffn_post[layer], g_next)
    return x2.reshape(b, t, d)
```

```python
import functools
import math

import jax
import jax.numpy as jnp
from jax import lax
from jax.experimental import pallas as pl
from jax.experimental.pallas import tpu as pltpu

HEAD_DIM = 128
A_PATTERNS = ((128, 1), (512, 4), (2048, 16))
A_GROUPS = len(A_PATTERNS)
A_HEADS_PER_GROUP = 4
A_HEADS = A_GROUPS * A_HEADS_PER_GROUP
B_HEADS = 8
N_A = 3 * A_HEADS * HEAD_DIM
N_B = 3 * B_HEADS * HEAD_DIM
N_QKV = N_A + N_B
N_BUCKETS = 32
REL_MAX_DIST = 2048
CONV_WIDTH = 3
Q_BLOCK = 128
EPS = 1e-6
NEG = -1e30
SCALE = HEAD_DIM ** -0.5

LANES = 128
SUBLANES = 8
V7X_VMEM_BYTES = 64 * 1024 * 1024
VMEM_BUDGET_BYTES = V7X_VMEM_BYTES - 8 * 1024 * 1024

F32 = jnp.float32
BF16 = jnp.bfloat16


def _params(semantics, vmem_bytes):
    return pltpu.CompilerParams(
        dimension_semantics=semantics,
        vmem_limit_bytes=int(min(max(vmem_bytes, 16 * 1024 * 1024), VMEM_BUDGET_BYTES)),
    )


def _pick(n, candidates):
    for c in candidates:
        if c <= n and n % c == 0:
            return c
    return n


def _rms(x, g):
    return x * lax.rsqrt(jnp.mean(x * x, axis=-1, keepdims=True) + EPS) * g


def _norm_kernel(x_ref, g_ref, h_ref):
    h_ref[...] = _rms(x_ref[...], g_ref[...]).astype(h_ref.dtype)


def _pre_norm(x2, g):
    m, d = x2.shape
    bm = _pick(m, (512, 256, 128))
    return pl.pallas_call(
        _norm_kernel,
        grid=(m // bm,),
        in_specs=[pl.BlockSpec((bm, d), lambda i: (i, 0)), pl.BlockSpec((1, d), lambda i: (0, 0))],
        out_specs=pl.BlockSpec((bm, d), lambda i: (i, 0)),
        out_shape=jax.ShapeDtypeStruct((m, d), BF16),
        compiler_params=_params(("arbitrary",), 6 * bm * d * 4),
        name="pre_norm",
    )(x2, g.reshape(1, d))


def _mm_kernel(a_ref, w_ref, o_ref):
    o_ref[...] = jnp.dot(a_ref[...], w_ref[...], preferred_element_type=F32).astype(o_ref.dtype)


def _mm_cast_kernel(a_ref, w_ref, o_ref):
    w = w_ref[...].astype(BF16)
    o_ref[...] = jnp.dot(a_ref[...], w, preferred_element_type=F32).astype(o_ref.dtype)


def _matmul(a, w, name):
    m, k = a.shape
    n = w.shape[1]
    bm = _pick(m, (1024, 512, 256, 128))
    bn = _pick(n, (1536, 1024, 512, 256, 128))
    vmem = 2 * 2 * (bm * k + k * bn + bm * bn) + 4 * bm * bn
    return pl.pallas_call(
        _mm_kernel,
        grid=(n // bn, m // bm),
        in_specs=[pl.BlockSpec((bm, k), lambda j, i: (i, 0)), pl.BlockSpec((k, bn), lambda j, i: (0, j))],
        out_specs=pl.BlockSpec((bm, bn), lambda j, i: (i, j)),
        out_shape=jax.ShapeDtypeStruct((m, n), BF16),
        compiler_params=_params(("arbitrary", "arbitrary"), vmem + vmem // 4),
        name=name,
    )(a, w)


def _matmul_f32w(a, w_all, layer, n, name):
    m, k = a.shape
    bm = _pick(m, (1024, 512, 256, 128))
    bn = _pick(n, (1280, 1024, 768, 512, 256, 128))
    vmem = 2 * (2 * bm * k + 4 * k * bn + 2 * bm * bn) + 2 * k * bn + 2 * 4 * bm * bn
    return pl.pallas_call(
        _mm_cast_kernel,
        grid=(n // bn, m // bm),
        in_specs=[pl.BlockSpec((bm, k), lambda j, i: (i, 0)),
                  pl.BlockSpec((None, k, bn), lambda j, i: (layer, 0, j))],
        out_specs=pl.BlockSpec((bm, bn), lambda j, i: (i, j)),
        out_shape=jax.ShapeDtypeStruct((m, n), BF16),
        compiler_params=_params(("arbitrary", "arbitrary"), vmem),
        name=name,
    )(a, w_all)


def _lane_cumsum(x):
    n = x.shape[1]
    lane = lax.broadcasted_iota(jnp.int32, x.shape, 1)
    s = 1
    while s < n:
        x = x + jnp.where(lane >= s, pltpu.roll(x, s, 1), 0.0)
        s *= 2
    return x


def _forget_kernel(h_ref, wft_ref, bf_ref, crow_ref, ccol_ref):
    ft = lax.dot_general(wft_ref[...], h_ref[0], (((1,), (1,)), ((), ())), preferred_element_type=F32)
    z = ft + bf_ref[...]
    logf = jnp.minimum(z, 0.0) - jnp.log1p(jnp.exp(-jnp.abs(z)))
    c = _lane_cumsum(logf)
    crow_ref[0] = c[:B_HEADS]
    ccol_ref[0] = c.T


def _forget(h3, wft, bf_col):
    b, t, d = h3.shape
    return pl.pallas_call(
        _forget_kernel,
        grid=(b,),
        in_specs=[
            pl.BlockSpec((1, t, d), lambda i: (i, 0, 0)),
            pl.BlockSpec((LANES, d), lambda i: (0, 0)),
            pl.BlockSpec((LANES, 1), lambda i: (0, 0)),
        ],
        out_specs=[
            pl.BlockSpec((1, B_HEADS, t), lambda i: (i, 0, 0)),
            pl.BlockSpec((1, t, LANES), lambda i: (i, 0, 0)),
        ],
        out_shape=[
            jax.ShapeDtypeStruct((b, B_HEADS, t), F32),
            jax.ShapeDtypeStruct((b, t, LANES), F32),
        ],
        compiler_params=_params(("arbitrary",), 2 * 2 * t * d + 16 * t * LANES * 4),
        name="forget_cumsum",
    )(h3, wft, bf_col)


def _bias_kernel(tbl_ref, o_ref):
    hd = pl.program_id(0)
    grp = hd // A_HEADS_PER_GROUP
    dil = jnp.where(grp == 0, A_PATTERNS[0][1], jnp.where(grp == 1, A_PATTERNS[1][1], A_PATTERNS[2][1]))
    shape = (Q_BLOCK, 2 * Q_BLOCK)
    qi = lax.broadcasted_iota(jnp.int32, shape, 0)
    kj = lax.broadcasted_iota(jnp.int32, shape, 1)
    delta = qi + Q_BLOCK - kj
    dist = jnp.maximum(delta, 0) * dil
    max_exact = N_BUCKETS // 2
    nf = jnp.maximum(dist, 1).astype(F32)
    large = max_exact + (jnp.log(nf / max_exact) / math.log(REL_MAX_DIST / max_exact)
                         * (N_BUCKETS - max_exact)).astype(jnp.int32)
    large = jnp.minimum(large, N_BUCKETS - 1)
    bucket = jnp.where(dist < max_exact, dist, large)
    bias = jnp.zeros(shape, F32)
    for i in range(N_BUCKETS):
        bias = jnp.where(bucket == i, tbl_ref[i, hd], bias)
    span = Q_BLOCK
    valid = (delta >= 0) & (delta <= span)
    o_ref[0] = jnp.where(valid, bias, NEG)


def _bias_tiles(rel_bias):
    return pl.pallas_call(
        _bias_kernel,
        grid=(A_HEADS,),
        in_specs=[pl.BlockSpec(memory_space=pltpu.SMEM)],
        out_specs=pl.BlockSpec((1, Q_BLOCK, 2 * Q_BLOCK), lambda i: (i, 0, 0)),
        out_shape=jax.ShapeDtypeStruct((A_HEADS, Q_BLOCK, 2 * Q_BLOCK), F32),
        compiler_params=_params(("arbitrary",), 0),
        name="rel_bias_tiles",
    )(rel_bias)


STRIDE = 4


def _dilated_kernel(*refs, t):
    qkv_refs = refs[:9]
    bias_refs = refs[9:12]
    o_ref = refs[12]
    stage, stage2 = refs[13:15]
    qk_bufs = refs[15:19]
    vaugs = refs[19:22]
    o_s = refs[22:25]
    lw_s = refs[25:28]
    nblk = t // Q_BLOCK

    for vaug in vaugs:
        vaug[:, HEAD_DIM:] = jnp.ones((t, HEAD_DIM), BF16)

    def deinterleave(src_ref, dil, write):
        stage[...] = src_ref[0].astype(F32)
        quarter = t // STRIDE
        if dil == STRIDE:
            for r in range(STRIDE):
                write(r * quarter, stage[pl.ds(r, quarter, stride=STRIDE), :])
        else:
            for r in range(STRIDE):
                stage2[r * quarter:(r + 1) * quarter, :] = stage[pl.ds(r, quarter, stride=STRIDE), :]
            sub = quarter // STRIDE
            for r in range(STRIDE):
                for j in range(STRIDE):
                    write((r + STRIDE * j) * sub, stage2[pl.ds(r * quarter + j, sub, stride=STRIDE), :])

    for g, (window, dil) in enumerate(A_PATTERNS):
        assert window // dil == Q_BLOCK and dil in (1, STRIDE, STRIDE ** 2)
        q_ref, k_ref, v_ref = qkv_refs[3 * g:3 * g + 3]
        vaug = vaugs[g]
        length = t // dil
        nb = length // Q_BLOCK
        if dil > 1:
            qd, kd = qk_bufs[2 * (g - 1):2 * g]

            def to_q(row0, x, qd=qd):
                qd[row0:row0 + x.shape[0], :] = x.astype(BF16)

            def to_k(row0, x, kd=kd):
                kd[row0:row0 + x.shape[0], :] = x.astype(BF16)

            def to_v(row0, x, vaug=vaug):
                vaug[row0:row0 + x.shape[0], :HEAD_DIM] = x.astype(BF16)

            deinterleave(q_ref, dil, to_q)
            deinterleave(k_ref, dil, to_k)
            deinterleave(v_ref, dil, to_v)
        else:
            vaug[:, :HEAD_DIM] = v_ref[0]
        bias = bias_refs[g][0]
        for blk in range(nblk):
            r, n = divmod(blk, nb)
            lo, hi = blk * Q_BLOCK, (blk + 1) * Q_BLOCK
            klo = lo if n == 0 else lo - Q_BLOCK
            if dil > 1:
                qn, kw = qd[lo:hi, :], kd[klo:hi, :]
            else:
                qn, kw = q_ref[0, lo:hi, :], k_ref[0, klo:hi, :]
            bt = bias[:, Q_BLOCK:] if n == 0 else bias
            s = lax.dot_general(qn, kw, (((1,), (1,)), ((), ())), preferred_element_type=F32) * SCALE + bt
            m = jnp.max(s, axis=-1, keepdims=True)
            p = jnp.exp(s - m)
            acc = jnp.dot(p.astype(BF16), vaug[klo:hi, :], preferred_element_type=F32)
            l = acc[:, HEAD_DIM:]
            if dil > 1:
                rows = pl.ds(n * Q_BLOCK * dil + r, Q_BLOCK, stride=dil)
            else:
                rows = pl.ds(lo, Q_BLOCK)
            o_s[g][rows, :] = acc[:, :HEAD_DIM] / l
            lw_s[g][rows, :] = m + jnp.log(l)

    chunk = 256

    def merge(c, carry):
        rows = pl.ds(pl.multiple_of(c * chunk, chunk), chunk)
        lws = [lw_s[g][rows, :] for g in range(A_GROUPS)]
        top = jnp.maximum(jnp.maximum(lws[0], lws[1]), lws[2])
        num = jnp.zeros((chunk, HEAD_DIM), F32)
        den = jnp.zeros((chunk, HEAD_DIM), F32)
        for g in range(A_GROUPS):
            w = jnp.exp(lws[g] - top)
            num = num + w * o_s[g][rows, :]
            den = den + w
        o_ref[0, rows, :] = (num / den).astype(o_ref.dtype)
        return carry

    lax.fori_loop(0, t // chunk, merge, 0)


def _dilated(proj3, bias_tiles):
    b, t, _ = proj3.shape
    hpg = A_HEADS_PER_GROUP

    def col(kind, g):
        return lambda i, h: (i, 0, kind * A_HEADS + g * hpg + h)

    in_specs = []
    for g in range(A_GROUPS):
        for kind in range(3):
            in_specs.append(pl.BlockSpec((1, t, HEAD_DIM), col(kind, g)))
    for g in range(A_GROUPS):
        in_specs.append(pl.BlockSpec((1, Q_BLOCK, 2 * Q_BLOCK), lambda i, h, g=g: (g * hpg + h, 0, 0)))
    plane = t * HEAD_DIM
    assert A_PATTERNS[0][1] == 1 and all(dil > 1 for _, dil in A_PATTERNS[1:])
    scratch = [pltpu.VMEM((t, HEAD_DIM), F32)] * 2 + [pltpu.VMEM((t, HEAD_DIM), BF16)] * 4
    scratch += [pltpu.VMEM((t, 2 * HEAD_DIM), BF16)] * 3 + [pltpu.VMEM((t, HEAD_DIM), F32)] * 6
    vmem = 2 * 10 * plane * 2 + 8 * plane * 4 + 10 * plane * 2 + 16 * 1024 * 1024
    return pl.pallas_call(
        functools.partial(_dilated_kernel, t=t),
        grid=(b, hpg),
        in_specs=in_specs,
        out_specs=pl.BlockSpec((1, t, HEAD_DIM), lambda i, h: (i, 0, h)),
        out_shape=jax.ShapeDtypeStruct((b, t, hpg * HEAD_DIM), BF16),
        scratch_shapes=scratch,
        compiler_params=_params(("arbitrary", "arbitrary"), vmem),
        name="dilated_attention",
    )(*([proj3] * 9), *([bias_tiles] * 3))


LOG2E = math.log2(math.e)
FOX_HEADS_PER_STEP = 4
FOX_TQ = 512
FOX_TK = 512


def _fox_kernel(q_ref, k_ref, v_ref, crow_ref, ccol_ref, o_ref, vaug_ref, *, tq, tk, nh):
    hb = pl.program_id(1)
    qi = pl.program_id(2)
    wide = 2 * HEAD_DIM

    @pl.when(qi == 0)
    def _():
        for i in range(nh):
            vaug_ref[:, i * wide:i * wide + HEAD_DIM] = v_ref[0, :, i * HEAD_DIM:(i + 1) * HEAD_DIM]
            vaug_ref[:, i * wide + HEAD_DIM:(i + 1) * wide] = jnp.ones((vaug_ref.shape[0], HEAD_DIM), BF16)

    lane = lax.broadcasted_iota(jnp.int32, (tq, LANES), 1)
    ccol = ccol_ref[0]
    qs, cts = [], []
    for i in range(nh):
        qs.append(q_ref[0, :, i * HEAD_DIM:(i + 1) * HEAD_DIM])
        c_t = jnp.sum(jnp.where(lane == hb * nh + i, ccol, 0.0), axis=1, keepdims=True)
        cts.append(c_t * LOG2E)

    def tile(j, carry, diagonal):
        start = pl.multiple_of(j * tk, tk)
        out = []
        for i in range(nh):
            m, acc = carry[i]
            k = k_ref[0, pl.ds(start, tk), i * HEAD_DIM:(i + 1) * HEAD_DIM]
            va = vaug_ref[pl.ds(start, tk), i * wide:(i + 1) * wide]
            c_s = crow_ref[0, pl.ds(hb * nh + i, 1), pl.ds(start, tk)] * LOG2E
            z = lax.dot_general(qs[i], k, (((1,), (1,)), ((), ())), preferred_element_type=F32)
            a = z * (SCALE * LOG2E) - c_s
            if diagonal:
                row = lax.broadcasted_iota(jnp.int32, (tq, tk), 0)
                colv = lax.broadcasted_iota(jnp.int32, (tq, tk), 1)
                a = jnp.where(colv - row <= qi * tq - j * tk, a, NEG)
            m_new = jnp.maximum(m, jnp.max(a, axis=-1, keepdims=True) + cts[i])
            alpha = jnp.exp2(m - m_new)
            p = jnp.exp2(a + (cts[i] - m_new))
            acc = alpha * acc + jnp.dot(p.astype(BF16), va, preferred_element_type=F32)
            out.append((m_new, acc))
        return tuple(out)

    init = tuple((jnp.full((tq, 1), NEG, F32), jnp.zeros((tq, wide), F32)) for _ in range(nh))
    n_full = (qi * tq) // tk
    carry = lax.fori_loop(0, n_full, lambda j, c: tile(j, c, False), init)
    carry = tile(n_full, carry, True)
    for i in range(nh):
        _, acc = carry[i]
        o_ref[0, :, i * HEAD_DIM:(i + 1) * HEAD_DIM] = (acc[:, :HEAD_DIM] / acc[:, HEAD_DIM:]).astype(o_ref.dtype)


def _fox(proj3, c_row, c_col):
    b, t, _ = proj3.shape
    nh = FOX_HEADS_PER_STEP
    tq = _pick(t, (FOX_TQ, 128))
    tk = _pick(t, (FOX_TK, 128))
    assert tk % tq == 0
    width = nh * HEAD_DIM
    qoff = N_A // width
    koff = qoff + B_HEADS // nh
    voff = koff + B_HEADS // nh
    assert N_A % width == 0 and B_HEADS % nh == 0
    return pl.pallas_call(
        functools.partial(_fox_kernel, tq=tq, tk=tk, nh=nh),
        grid=(b, B_HEADS // nh, t // tq),
        in_specs=[
            pl.BlockSpec((1, tq, width), lambda i, h, q: (i, q, qoff + h)),
            pl.BlockSpec((1, t, width), lambda i, h, q: (i, 0, koff + h)),
            pl.BlockSpec((1, t, width), lambda i, h, q: (i, 0, voff + h)),
            pl.BlockSpec((1, B_HEADS, t), lambda i, h, q: (i, 0, 0)),
            pl.BlockSpec((1, tq, LANES), lambda i, h, q: (i, q, 0)),
        ],
        out_specs=pl.BlockSpec((1, tq, width), lambda i, h, q: (i, q, h)),
        out_shape=jax.ShapeDtypeStruct((b, t, B_HEADS * HEAD_DIM), BF16),
        scratch_shapes=[pltpu.VMEM((t, 2 * width), BF16)],
        compiler_params=_params(("arbitrary", "arbitrary", "arbitrary"), 8 * t * width * 2 + 16 * 1024 * 1024),
        name="fox_attention",
    )(proj3, proj3, proj3, c_row, c_col)


MIX_ROW_CHUNKS = 2
DOWN_ROW_CHUNKS = 4


def _mix_kernel(ya_ref, yb_ref, ga_ref, gb_ref, wpa_ref, wpb_ref, wo_ref, x_ref, gpost_ref, gpre_ref,
                xo_ref, h_ref):
    rows = ya_ref.shape[0] // MIX_ROW_CHUNKS
    for r in range(MIX_ROW_CHUNKS):
        sl = slice(r * rows, (r + 1) * rows)
        pa = jnp.dot(ya_ref[sl, :], wpa_ref[...], preferred_element_type=F32)
        pb = jnp.dot(yb_ref[sl, :], wpb_ref[...], preferred_element_type=F32)
        ga = jax.nn.sigmoid(ga_ref[sl, :].astype(F32))
        gb = jax.nn.sigmoid(gb_ref[sl, :].astype(F32))
        merged = (ga * pa + gb * pb).astype(BF16)
        mo = jnp.dot(merged, wo_ref[...], preferred_element_type=F32)
        x_new = x_ref[sl, :] + _rms(mo, gpost_ref[...])
        xo_ref[sl, :] = x_new
        h_ref[sl, :] = _rms(x_new, gpre_ref[...]).astype(h_ref.dtype)


def _mix(ya, yb, gates, wpa, wpb, wo, x2, g_post, g_pre):
    m, d = x2.shape
    ka, kb = ya.shape[1], yb.shape[1]
    bm = _pick(m, (512, 256, 128))
    const = lambda i: (0, 0)
    single = dict(pipeline_mode=pl.Buffered(1))
    vmem = 2 * (ka + kb + d) * d + 2 * bm * (2 * (ka + kb + 2 * d) + 4 * d + 4 * d + 2 * d) + 8 * bm * d * 4
    return pl.pallas_call(
        _mix_kernel,
        grid=(m // bm,),
        in_specs=[
            pl.BlockSpec((bm, ka), lambda i: (i, 0)),
            pl.BlockSpec((bm, kb), lambda i: (i, 0)),
            pl.BlockSpec((bm, d), lambda i: (i, 0)),
            pl.BlockSpec((bm, d), lambda i: (i, 1)),
            pl.BlockSpec((ka, d), const, **single),
            pl.BlockSpec((kb, d), const, **single),
            pl.BlockSpec((d, d), const, **single),
            pl.BlockSpec((bm, d), lambda i: (i, 0)),
            pl.BlockSpec((1, d), const),
            pl.BlockSpec((1, d), const),
        ],
        out_specs=[pl.BlockSpec((bm, d), lambda i: (i, 0)), pl.BlockSpec((bm, d), lambda i: (i, 0))],
        out_shape=[jax.ShapeDtypeStruct((m, d), F32), jax.ShapeDtypeStruct((m, d), BF16)],
        compiler_params=_params(("arbitrary",), vmem),
        name="gated_mix",
    )(ya, yb, gates, gates, wpa, wpb, wo, x2, g_post.reshape(1, d), g_pre.reshape(1, d))


def _ffn_up_kernel(h_ref, wg_ref, wv_ref, cwg_ref, cwv_ref, cbg_ref, cbv_ref, o_ref, carry_g, carry_v,
                   *, blocks_per_seq):
    i = pl.program_id(1)
    bm = h_ref.shape[0]

    @pl.when(i % blocks_per_seq == 0)
    def _():
        carry_g[...] = jnp.zeros_like(carry_g)
        carry_v[...] = jnp.zeros_like(carry_v)

    def conv3(u, carry_ref, cw, cb):
        ext = jnp.concatenate([carry_ref[...], u], axis=0)
        x1 = pltpu.roll(ext, 1, 0)[SUBLANES:]
        x2 = pltpu.roll(ext, 2, 0)[SUBLANES:]
        carry_ref[...] = u[bm - SUBLANES:]
        return cw[0:1] * x2 + cw[1:2] * x1 + cw[2:3] * u + cb

    h = h_ref[...]
    ug = jnp.dot(h, wg_ref[...].astype(BF16), preferred_element_type=F32)
    uv = jnp.dot(h, wv_ref[...].astype(BF16), preferred_element_type=F32)
    gate = conv3(ug, carry_g, cwg_ref[...], cbg_ref[...])
    val = conv3(uv, carry_v, cwv_ref[...], cbv_ref[...])
    cdf = 0.5 * (1.0 + jnp.tanh(math.sqrt(2.0 / math.pi) * (gate + 0.044715 * (gate * gate * gate))))
    o_ref[...] = (gate * cdf * val).astype(o_ref.dtype)


def _ffn_up(h2, w_up_all, layer, conv_w, conv_b, t):
    m, d = h2.shape
    dff = w_up_all.shape[2] // 2
    bn = _pick(dff, (512, 256, 128))
    nj = dff // bn
    bm = _pick(t, (1024, 512, 256, 128))
    vmem = 2 * (2 * bm * d + 2 * 4 * d * bn + 2 * bm * bn) + 2 * 2 * d * bn + 32 * bm * bn
    return pl.pallas_call(
        functools.partial(_ffn_up_kernel, blocks_per_seq=t // bm),
        grid=(nj, m // bm),
        in_specs=[
            pl.BlockSpec((bm, d), lambda j, i: (i, 0)),
            pl.BlockSpec((None, d, bn), lambda j, i: (layer, 0, j)),
            pl.BlockSpec((None, d, bn), lambda j, i: (layer, 0, nj + j)),
            pl.BlockSpec((CONV_WIDTH, bn), lambda j, i: (0, j)),
            pl.BlockSpec((CONV_WIDTH, bn), lambda j, i: (0, nj + j)),
            pl.BlockSpec((1, bn), lambda j, i: (0, j)),
            pl.BlockSpec((1, bn), lambda j, i: (0, nj + j)),
        ],
        out_specs=pl.BlockSpec((bm, bn), lambda j, i: (i, j)),
        out_shape=jax.ShapeDtypeStruct((m, dff), BF16),
        scratch_shapes=[pltpu.VMEM((SUBLANES, bn), F32), pltpu.VMEM((SUBLANES, bn), F32)],
        compiler_params=_params(("arbitrary", "arbitrary"), vmem),
        name="ffn_up_conv_act",
    )(h2, w_up_all, w_up_all, conv_w, conv_w, conv_b.reshape(1, -1), conv_b.reshape(1, -1))


def _ffn_down_kernel(a_ref, w_ref, x_ref, gpost_ref, gpre_ref, xo_ref, *h_ref):
    rows = a_ref.shape[0] // DOWN_ROW_CHUNKS
    for r in range(DOWN_ROW_CHUNKS):
        sl = slice(r * rows, (r + 1) * rows)
        y = jnp.dot(a_ref[sl, :], w_ref[...], preferred_element_type=F32)
        x_new = x_ref[sl, :] + _rms(y, gpost_ref[...])
        xo_ref[sl, :] = x_new
        if h_ref:
            h_ref[0][sl, :] = _rms(x_new, gpre_ref[...]).astype(h_ref[0].dtype)


def _ffn_down(act, w_down, x2, g_post, g_pre_next):
    m, d = x2.shape
    dff = act.shape[1]
    bm = _pick(m, (512, 256, 128))
    with_h = g_pre_next is not None
    g_pre = g_pre_next if with_h else g_post
    out_specs = [pl.BlockSpec((bm, d), lambda i: (i, 0))]
    out_shape = [jax.ShapeDtypeStruct((m, d), F32)]
    if with_h:
        out_specs.append(pl.BlockSpec((bm, d), lambda i: (i, 0)))
        out_shape.append(jax.ShapeDtypeStruct((m, d), BF16))
    vmem = 2 * dff * d + 2 * bm * (2 * dff + 4 * d + 4 * d + 2 * d) + 6 * bm * d * 4
    outs = pl.pallas_call(
        _ffn_down_kernel,
        grid=(m // bm,),
        in_specs=[
            pl.BlockSpec((bm, dff), lambda i: (i, 0)),
            pl.BlockSpec((dff, d), lambda i: (0, 0), pipeline_mode=pl.Buffered(1)),
            pl.BlockSpec((bm, d), lambda i: (i, 0)),
            pl.BlockSpec((1, d), lambda i: (0, 0)),
            pl.BlockSpec((1, d), lambda i: (0, 0)),
        ],
        out_specs=out_specs,
        out_shape=out_shape,
        compiler_params=_params(("arbitrary",), vmem),
        name="ffn_down_norm",
    )(act, w_down, x2, g_post.reshape(1, d), g_pre.reshape(1, d))
    return (outs[0], outs[1]) if with_h else (outs[0], None)


def kernel(x, rel_bias, w_in, b_f, w_pa, w_pb, w_o, w_up, conv_w, conv_b, w_down,
           g_mix_pre, g_mix_post, g_ffn_pre, g_ffn_post):
    b, t, d = x.shape
    depth = w_in.shape[0]
    m = b * t
    assert t % max(dil for _, dil in A_PATTERNS) == 0 and (t // A_PATTERNS[-1][1]) % Q_BLOCK == 0
    assert w_in.shape[2] == N_QKV + B_HEADS + 2 * d

    bias_tiles = _bias_tiles(rel_bias)
    x2 = x.reshape(m, d)
    h = _pre_norm(x2, g_mix_pre[0])
    for layer in range(depth):
        wl = w_in[layer]
        w_f = wl[:, N_QKV:N_QKV + B_HEADS]
        w_gate = wl[:, N_QKV + B_HEADS:].astype(BF16)
        wft = jnp.zeros((LANES, d), BF16).at[:B_HEADS].set(w_f.T.astype(BF16))
        bf_col = jnp.zeros((LANES, 1), F32).at[:B_HEADS, 0].set(b_f[layer])

        proj = _matmul_f32w(h, w_in, layer, N_QKV, "proj_qkv")
        gates = _matmul(h, w_gate, "proj_gates")
        c_row, c_col = _forget(h.reshape(b, t, d), wft, bf_col)
        proj3 = proj.reshape(b, t, N_QKV)
        y_a = _dilated(proj3, bias_tiles)
        y_b = _fox(proj3, c_row, c_col)
        x2, h2 = _mix(y_a.reshape(m, -1), y_b.reshape(m, -1), gates,
                      w_pa[layer].astype(BF16), w_pb[layer].astype(BF16), w_o[layer].astype(BF16),
                      x2, g_mix_post[layer], g_ffn_pre[layer])
        act = _ffn_up(h2, w_up, layer, conv_w[layer], conv_b[layer], t)
        g_next = g_mix_pre[layer + 1] if layer + 1 < depth else None
        x2, h = _ffn_down(act, w_down[layer].astype(BF16), x2, g_ffn_post[layer], g_next)
    return x2.reshape(b, t, d)
```

```python
import functools
import math

import jax
import jax.numpy as jnp
from jax import lax
from jax.experimental import pallas as pl
from jax.experimental.pallas import tpu as pltpu

HEAD_DIM = 128
A_PATTERNS = ((128, 1), (512, 4), (2048, 16))
A_GROUPS = len(A_PATTERNS)
A_HEADS_PER_GROUP = 4
A_HEADS = A_GROUPS * A_HEADS_PER_GROUP
B_HEADS = 8
N_A = 3 * A_HEADS * HEAD_DIM
N_B = 3 * B_HEADS * HEAD_DIM
N_QKV = N_A + N_B
N_BUCKETS = 32
REL_MAX_DIST = 2048
CONV_WIDTH = 3
Q_BLOCK = 128
EPS = 1e-6
NEG = -1e30
SCALE = HEAD_DIM ** -0.5

LANES = 128
SUBLANES = 8
V7X_VMEM_BYTES = 64 * 1024 * 1024
VMEM_BUDGET_BYTES = V7X_VMEM_BYTES - 8 * 1024 * 1024

F32 = jnp.float32
BF16 = jnp.bfloat16


def _params(semantics, vmem_bytes):
    return pltpu.CompilerParams(
        dimension_semantics=semantics,
        vmem_limit_bytes=int(min(max(vmem_bytes, 16 * 1024 * 1024), VMEM_BUDGET_BYTES)),
    )


def _pick(n, candidates):
    for c in candidates:
        if c <= n and n % c == 0:
            return c
    return n


def _rms(x, g):
    return x * lax.rsqrt(jnp.mean(x * x, axis=-1, keepdims=True) + EPS) * g


def _norm_kernel(x_ref, g_ref, h_ref):
    h_ref[...] = _rms(x_ref[...], g_ref[...]).astype(h_ref.dtype)


def _pre_norm(x2, g):
    m, d = x2.shape
    bm = _pick(m, (512, 256, 128))
    return pl.pallas_call(
        _norm_kernel,
        grid=(m // bm,),
        in_specs=[pl.BlockSpec((bm, d), lambda i: (i, 0)), pl.BlockSpec((1, d), lambda i: (0, 0))],
        out_specs=pl.BlockSpec((bm, d), lambda i: (i, 0)),
        out_shape=jax.ShapeDtypeStruct((m, d), BF16),
        compiler_params=_params(("arbitrary",), 6 * bm * d * 4),
        name="pre_norm",
    )(x2, g.reshape(1, d))


def _mm_nt_kernel(a_ref, wt_ref, o_ref):
    wt = wt_ref[0].astype(BF16)
    o_ref[...] = lax.dot_general(a_ref[...], wt, (((1,), (1,)), ((), ())),
                                 preferred_element_type=F32).astype(o_ref.dtype)


def _matmul_wt(a, wt_all, layer, row0, n, name):
    m, k = a.shape
    assert row0 % SUBLANES == 0
    bm = _pick(m, (1024, 512, 256, 128))
    bn = _pick(n, (1280, 1024, 768, 512, 256, 128))
    vmem = 2 * (2 * bm * k + 4 * k * bn + 2 * bm * bn) + 2 * k * bn + 2 * 4 * bm * bn
    wspec = pl.BlockSpec((pl.Element(1), pl.Element(bn), pl.Element(k)),
                         lambda j, i: (layer, pl.multiple_of(row0 + j * bn, SUBLANES), 0))
    return pl.pallas_call(
        _mm_nt_kernel,
        grid=(n // bn, m // bm),
        in_specs=[pl.BlockSpec((bm, k), lambda j, i: (i, 0)), wspec],
        out_specs=pl.BlockSpec((bm, bn), lambda j, i: (i, j)),
        out_shape=jax.ShapeDtypeStruct((m, n), BF16),
        compiler_params=_params(("arbitrary", "arbitrary"), vmem),
        name=name,
    )(a, wt_all)


def _lane_cumsum(x):
    n = x.shape[1]
    lane = lax.broadcasted_iota(jnp.int32, x.shape, 1)
    s = 1
    while s < n:
        x = x + jnp.where(lane >= s, pltpu.roll(x, s, 1), 0.0)
        s *= 2
    return x


def _forget_kernel(h_ref, wft_ref, bf_ref, crow_ref, ccol_ref):
    d = wft_ref.shape[-1]
    wft = jnp.concatenate([wft_ref[0].astype(BF16), jnp.zeros((LANES - B_HEADS, d), BF16)], axis=0)
    ft = lax.dot_general(wft, h_ref[0], (((1,), (1,)), ((), ())), preferred_element_type=F32)
    z = ft + bf_ref[...]
    logf = jnp.minimum(z, 0.0) - jnp.log1p(jnp.exp(-jnp.abs(z)))
    c = _lane_cumsum(logf)
    crow_ref[0] = c[:B_HEADS]
    ccol_ref[0] = c.T


def _forget(h3, wt_all, layer, row0, bf_col):
    b, t, d = h3.shape
    assert row0 % SUBLANES == 0 and B_HEADS == SUBLANES
    return pl.pallas_call(
        _forget_kernel,
        grid=(b,),
        in_specs=[
            pl.BlockSpec((1, t, d), lambda i: (i, 0, 0)),
            pl.BlockSpec((pl.Element(1), pl.Element(B_HEADS), pl.Element(d)), lambda i: (layer, row0, 0)),
            pl.BlockSpec((LANES, 1), lambda i: (0, 0)),
        ],
        out_specs=[
            pl.BlockSpec((1, B_HEADS, t), lambda i: (i, 0, 0)),
            pl.BlockSpec((1, t, LANES), lambda i: (i, 0, 0)),
        ],
        out_shape=[
            jax.ShapeDtypeStruct((b, B_HEADS, t), F32),
            jax.ShapeDtypeStruct((b, t, LANES), F32),
        ],
        compiler_params=_params(("arbitrary",), 2 * 2 * t * d + 16 * t * LANES * 4),
        name="forget_cumsum",
    )(h3, wt_all, bf_col)


def _bias_kernel(tbl_ref, o_ref):
    hd = pl.program_id(0)
    grp = hd // A_HEADS_PER_GROUP
    dil = jnp.where(grp == 0, A_PATTERNS[0][1], jnp.where(grp == 1, A_PATTERNS[1][1], A_PATTERNS[2][1]))
    shape = (Q_BLOCK, 2 * Q_BLOCK)
    qi = lax.broadcasted_iota(jnp.int32, shape, 0)
    kj = lax.broadcasted_iota(jnp.int32, shape, 1)
    delta = qi + Q_BLOCK - kj
    dist = jnp.maximum(delta, 0) * dil
    max_exact = N_BUCKETS // 2
    nf = jnp.maximum(dist, 1).astype(F32)
    large = max_exact + (jnp.log(nf / max_exact) / math.log(REL_MAX_DIST / max_exact)
                         * (N_BUCKETS - max_exact)).astype(jnp.int32)
    large = jnp.minimum(large, N_BUCKETS - 1)
    bucket = jnp.where(dist < max_exact, dist, large)
    bias = jnp.zeros(shape, F32)
    for i in range(N_BUCKETS):
        bias = jnp.where(bucket == i, tbl_ref[i, hd], bias)
    span = Q_BLOCK
    valid = (delta >= 0) & (delta <= span)
    o_ref[0] = jnp.where(valid, bias, NEG)


def _bias_tiles(rel_bias):
    return pl.pallas_call(
        _bias_kernel,
        grid=(A_HEADS,),
        in_specs=[pl.BlockSpec(memory_space=pltpu.SMEM)],
        out_specs=pl.BlockSpec((1, Q_BLOCK, 2 * Q_BLOCK), lambda i: (i, 0, 0)),
        out_shape=jax.ShapeDtypeStruct((A_HEADS, Q_BLOCK, 2 * Q_BLOCK), F32),
        compiler_params=_params(("arbitrary",), 0),
        name="rel_bias_tiles",
    )(rel_bias)


STRIDE = 4


def _dilated_kernel(*refs, t):
    qkv_refs = refs[:9]
    bias_refs = refs[9:12]
    o_ref = refs[12]
    stage, stage2 = refs[13:15]
    qk_bufs = refs[15:19]
    vaugs = refs[19:22]
    o_s = refs[22:25]
    lw_s = refs[25:28]
    nblk = t // Q_BLOCK

    for vaug in vaugs:
        vaug[:, HEAD_DIM:] = jnp.ones((t, HEAD_DIM), BF16)

    def deinterleave(src_ref, dil, write):
        stage[...] = src_ref[0].astype(F32)
        quarter = t // STRIDE
        if dil == STRIDE:
            for r in range(STRIDE):
                write(r * quarter, stage[pl.ds(r, quarter, stride=STRIDE), :])
        else:
            for r in range(STRIDE):
                stage2[r * quarter:(r + 1) * quarter, :] = stage[pl.ds(r, quarter, stride=STRIDE), :]
            sub = quarter // STRIDE
            for r in range(STRIDE):
                for j in range(STRIDE):
                    write((r + STRIDE * j) * sub, stage2[pl.ds(r * quarter + j, sub, stride=STRIDE), :])

    for g, (window, dil) in enumerate(A_PATTERNS):
        assert window // dil == Q_BLOCK and dil in (1, STRIDE, STRIDE ** 2)
        q_ref, k_ref, v_ref = qkv_refs[3 * g:3 * g + 3]
        vaug = vaugs[g]
        length = t // dil
        nb = length // Q_BLOCK
        if dil > 1:
            qd, kd = qk_bufs[2 * (g - 1):2 * g]

            def to_q(row0, x, qd=qd):
                qd[row0:row0 + x.shape[0], :] = x.astype(BF16)

            def to_k(row0, x, kd=kd):
                kd[row0:row0 + x.shape[0], :] = x.astype(BF16)

            def to_v(row0, x, vaug=vaug):
                vaug[row0:row0 + x.shape[0], :HEAD_DIM] = x.astype(BF16)

            deinterleave(q_ref, dil, to_q)
            deinterleave(k_ref, dil, to_k)
            deinterleave(v_ref, dil, to_v)
        else:
            vaug[:, :HEAD_DIM] = v_ref[0]
        bias = bias_refs[g][0]
        for blk in range(nblk):
            r, n = divmod(blk, nb)
            lo, hi = blk * Q_BLOCK, (blk + 1) * Q_BLOCK
            klo = lo if n == 0 else lo - Q_BLOCK
            if dil > 1:
                qn, kw = qd[lo:hi, :], kd[klo:hi, :]
            else:
                qn, kw = q_ref[0, lo:hi, :], k_ref[0, klo:hi, :]
            bt = bias[:, Q_BLOCK:] if n == 0 else bias
            s = lax.dot_general(qn, kw, (((1,), (1,)), ((), ())), preferred_element_type=F32) * SCALE + bt
            m = jnp.max(s, axis=-1, keepdims=True)
            p = jnp.exp(s - m)
            acc = jnp.dot(p.astype(BF16), vaug[klo:hi, :], preferred_element_type=F32)
            l = acc[:, HEAD_DIM:]
            if dil > 1:
                rows = pl.ds(n * Q_BLOCK * dil + r, Q_BLOCK, stride=dil)
            else:
                rows = pl.ds(lo, Q_BLOCK)
            o_s[g][rows, :] = acc[:, :HEAD_DIM] / l
            lw_s[g][rows, :] = m + jnp.log(l)

    chunk = 256

    def merge(c, carry):
        rows = pl.ds(pl.multiple_of(c * chunk, chunk), chunk)
        lws = [lw_s[g][rows, :] for g in range(A_GROUPS)]
        top = jnp.maximum(jnp.maximum(lws[0], lws[1]), lws[2])
        num = jnp.zeros((chunk, HEAD_DIM), F32)
        den = jnp.zeros((chunk, HEAD_DIM), F32)
        for g in range(A_GROUPS):
            w = jnp.exp(lws[g] - top)
            num = num + w * o_s[g][rows, :]
            den = den + w
        o_ref[0, rows, :] = (num / den).astype(o_ref.dtype)
        return carry

    lax.fori_loop(0, t // chunk, merge, 0)


def _dilated(proj3, bias_tiles):
    b, t, _ = proj3.shape
    hpg = A_HEADS_PER_GROUP

    def col(kind, g):
        return lambda i, h: (i, 0, kind * A_HEADS + g * hpg + h)

    in_specs = []
    for g in range(A_GROUPS):
        for kind in range(3):
            in_specs.append(pl.BlockSpec((1, t, HEAD_DIM), col(kind, g)))
    for g in range(A_GROUPS):
        in_specs.append(pl.BlockSpec((1, Q_BLOCK, 2 * Q_BLOCK), lambda i, h, g=g: (g * hpg + h, 0, 0)))
    plane = t * HEAD_DIM
    assert A_PATTERNS[0][1] == 1 and all(dil > 1 for _, dil in A_PATTERNS[1:])
    scratch = [pltpu.VMEM((t, HEAD_DIM), F32)] * 2 + [pltpu.VMEM((t, HEAD_DIM), BF16)] * 4
    scratch += [pltpu.VMEM((t, 2 * HEAD_DIM), BF16)] * 3 + [pltpu.VMEM((t, HEAD_DIM), F32)] * 6
    vmem = 2 * 10 * plane * 2 + 8 * plane * 4 + 10 * plane * 2 + 16 * 1024 * 1024
    return pl.pallas_call(
        functools.partial(_dilated_kernel, t=t),
        grid=(b, hpg),
        in_specs=in_specs,
        out_specs=pl.BlockSpec((1, t, HEAD_DIM), lambda i, h: (i, 0, h)),
        out_shape=jax.ShapeDtypeStruct((b, t, hpg * HEAD_DIM), BF16),
        scratch_shapes=scratch,
        compiler_params=_params(("arbitrary", "arbitrary"), vmem),
        name="dilated_attention",
    )(*([proj3] * 9), *([bias_tiles] * 3))


LOG2E = math.log2(math.e)
FOX_HEADS_PER_STEP = 4
FOX_TQ = 512
FOX_TK = 512


def _fox_kernel(q_ref, k_ref, v_ref, crow_ref, ccol_ref, o_ref, vaug_ref, *, tq, tk, nh):
    hb = pl.program_id(1)
    qi = pl.program_id(2)
    wide = 2 * HEAD_DIM

    @pl.when(qi == 0)
    def _():
        for i in range(nh):
            vaug_ref[:, i * wide:i * wide + HEAD_DIM] = v_ref[0, :, i * HEAD_DIM:(i + 1) * HEAD_DIM]
            vaug_ref[:, i * wide + HEAD_DIM:(i + 1) * wide] = jnp.ones((vaug_ref.shape[0], HEAD_DIM), BF16)

    lane = lax.broadcasted_iota(jnp.int32, (tq, LANES), 1)
    ccol = ccol_ref[0]
    qs, cts = [], []
    for i in range(nh):
        qs.append(q_ref[0, :, i * HEAD_DIM:(i + 1) * HEAD_DIM])
        c_t = jnp.sum(jnp.where(lane == hb * nh + i, ccol, 0.0), axis=1, keepdims=True)
        cts.append(c_t * LOG2E)

    def tile(j, carry, diagonal):
        start = pl.multiple_of(j * tk, tk)
        out = []
        for i in range(nh):
            m, acc = carry[i]
            k = k_ref[0, pl.ds(start, tk), i * HEAD_DIM:(i + 1) * HEAD_DIM]
            va = vaug_ref[pl.ds(start, tk), i * wide:(i + 1) * wide]
            c_s = crow_ref[0, pl.ds(hb * nh + i, 1), pl.ds(start, tk)] * LOG2E
            z = lax.dot_general(qs[i], k, (((1,), (1,)), ((), ())), preferred_element_type=F32)
            a = z * (SCALE * LOG2E) - c_s
            if diagonal:
                row = lax.broadcasted_iota(jnp.int32, (tq, tk), 0)
                colv = lax.broadcasted_iota(jnp.int32, (tq, tk), 1)
                a = jnp.where(colv - row <= qi * tq - j * tk, a, NEG)
            m_new = jnp.maximum(m, jnp.max(a, axis=-1, keepdims=True) + cts[i])
            alpha = jnp.exp2(m - m_new)
            p = jnp.exp2(a + (cts[i] - m_new))
            acc = alpha * acc + jnp.dot(p.astype(BF16), va, preferred_element_type=F32)
            out.append((m_new, acc))
        return tuple(out)

    init = tuple((jnp.full((tq, 1), NEG, F32), jnp.zeros((tq, wide), F32)) for _ in range(nh))
    n_full = (qi * tq) // tk
    carry = lax.fori_loop(0, n_full, lambda j, c: tile(j, c, False), init)
    carry = tile(n_full, carry, True)
    for i in range(nh):
        _, acc = carry[i]
        o_ref[0, :, i * HEAD_DIM:(i + 1) * HEAD_DIM] = (acc[:, :HEAD_DIM] / acc[:, HEAD_DIM:]).astype(o_ref.dtype)


def _fox(proj3, c_row, c_col):
    b, t, _ = proj3.shape
    nh = FOX_HEADS_PER_STEP
    tq = _pick(t, (FOX_TQ, 128))
    tk = _pick(t, (FOX_TK, 128))
    assert tk % tq == 0
    width = nh * HEAD_DIM
    qoff = N_A // width
    koff = qoff + B_HEADS // nh
    voff = koff + B_HEADS // nh
    assert N_A % width == 0 and B_HEADS % nh == 0
    return pl.pallas_call(
        functools.partial(_fox_kernel, tq=tq, tk=tk, nh=nh),
        grid=(b, B_HEADS // nh, t // tq),
        in_specs=[
            pl.BlockSpec((1, tq, width), lambda i, h, q: (i, q, qoff + h)),
            pl.BlockSpec((1, t, width), lambda i, h, q: (i, 0, koff + h)),
            pl.BlockSpec((1, t, width), lambda i, h, q: (i, 0, voff + h)),
            pl.BlockSpec((1, B_HEADS, t), lambda i, h, q: (i, 0, 0)),
            pl.BlockSpec((1, tq, LANES), lambda i, h, q: (i, q, 0)),
        ],
        out_specs=pl.BlockSpec((1, tq, width), lambda i, h, q: (i, q, h)),
        out_shape=jax.ShapeDtypeStruct((b, t, B_HEADS * HEAD_DIM), BF16),
        scratch_shapes=[pltpu.VMEM((t, 2 * width), BF16)],
        compiler_params=_params(("arbitrary", "arbitrary", "arbitrary"), 8 * t * width * 2 + 16 * 1024 * 1024),
        name="fox_attention",
    )(proj3, proj3, proj3, c_row, c_col)


MIX_ROW_CHUNKS = 2
DOWN_ROW_CHUNKS = 4


def _mix_kernel(ya_ref, yb_ref, ga_ref, gb_ref, wpa_ref, wpb_ref, wo_ref, x_ref, gpost_ref, gpre_ref,
                xo_ref, h_ref):
    rows = ya_ref.shape[0] // MIX_ROW_CHUNKS
    for r in range(MIX_ROW_CHUNKS):
        sl = slice(r * rows, (r + 1) * rows)
        pa = jnp.dot(ya_ref[sl, :], wpa_ref[...], preferred_element_type=F32)
        pb = jnp.dot(yb_ref[sl, :], wpb_ref[...], preferred_element_type=F32)
        ga = jax.nn.sigmoid(ga_ref[sl, :].astype(F32))
        gb = jax.nn.sigmoid(gb_ref[sl, :].astype(F32))
        merged = (ga * pa + gb * pb).astype(BF16)
        mo = jnp.dot(merged, wo_ref[...], preferred_element_type=F32)
        x_new = x_ref[sl, :] + _rms(mo, gpost_ref[...])
        xo_ref[sl, :] = x_new
        h_ref[sl, :] = _rms(x_new, gpre_ref[...]).astype(h_ref.dtype)


def _mix(ya, yb, gates, wpa, wpb, wo, x2, g_post, g_pre):
    m, d = x2.shape
    ka, kb = ya.shape[1], yb.shape[1]
    bm = _pick(m, (512, 256, 128))
    const = lambda i: (0, 0)
    single = dict(pipeline_mode=pl.Buffered(1))
    vmem = 2 * (ka + kb + d) * d + 2 * bm * (2 * (ka + kb + 2 * d) + 4 * d + 4 * d + 2 * d) + 8 * bm * d * 4
    return pl.pallas_call(
        _mix_kernel,
        grid=(m // bm,),
        in_specs=[
            pl.BlockSpec((bm, ka), lambda i: (i, 0)),
            pl.BlockSpec((bm, kb), lambda i: (i, 0)),
            pl.BlockSpec((bm, d), lambda i: (i, 0)),
            pl.BlockSpec((bm, d), lambda i: (i, 1)),
            pl.BlockSpec((ka, d), const, **single),
            pl.BlockSpec((kb, d), const, **single),
            pl.BlockSpec((d, d), const, **single),
            pl.BlockSpec((bm, d), lambda i: (i, 0)),
            pl.BlockSpec((1, d), const),
            pl.BlockSpec((1, d), const),
        ],
        out_specs=[pl.BlockSpec((bm, d), lambda i: (i, 0)), pl.BlockSpec((bm, d), lambda i: (i, 0))],
        out_shape=[jax.ShapeDtypeStruct((m, d), F32), jax.ShapeDtypeStruct((m, d), BF16)],
        compiler_params=_params(("arbitrary",), vmem),
        name="gated_mix",
    )(ya, yb, gates, gates, wpa, wpb, wo, x2, g_post.reshape(1, d), g_pre.reshape(1, d))


def _ffn_up_kernel(h_ref, wg_ref, wv_ref, cwg_ref, cwv_ref, cbg_ref, cbv_ref, o_ref, carry_g, carry_v,
                   *, blocks_per_seq):
    i = pl.program_id(1)
    bm = h_ref.shape[0]

    @pl.when(i % blocks_per_seq == 0)
    def _():
        carry_g[...] = jnp.zeros_like(carry_g)
        carry_v[...] = jnp.zeros_like(carry_v)

    def conv3(u, carry_ref, cw, cb):
        ext = jnp.concatenate([carry_ref[...], u], axis=0)
        x1 = pltpu.roll(ext, 1, 0)[SUBLANES:]
        x2 = pltpu.roll(ext, 2, 0)[SUBLANES:]
        carry_ref[...] = u[bm - SUBLANES:]
        return cw[0:1] * x2 + cw[1:2] * x1 + cw[2:3] * u + cb

    h = h_ref[...]
    ug = jnp.dot(h, wg_ref[...].astype(BF16), preferred_element_type=F32)
    uv = jnp.dot(h, wv_ref[...].astype(BF16), preferred_element_type=F32)
    gate = conv3(ug, carry_g, cwg_ref[...], cbg_ref[...])
    val = conv3(uv, carry_v, cwv_ref[...], cbv_ref[...])
    cdf = 0.5 * (1.0 + jnp.tanh(math.sqrt(2.0 / math.pi) * (gate + 0.044715 * (gate * gate * gate))))
    o_ref[...] = (gate * cdf * val).astype(o_ref.dtype)


def _ffn_up(h2, w_up_all, layer, conv_w, conv_b, t):
    m, d = h2.shape
    dff = w_up_all.shape[2] // 2
    bn = _pick(dff, (512, 256, 128))
    nj = dff // bn
    bm = _pick(t, (1024, 512, 256, 128))
    vmem = 2 * (2 * bm * d + 2 * 4 * d * bn + 2 * bm * bn) + 2 * 2 * d * bn + 32 * bm * bn
    return pl.pallas_call(
        functools.partial(_ffn_up_kernel, blocks_per_seq=t // bm),
        grid=(nj, m // bm),
        in_specs=[
            pl.BlockSpec((bm, d), lambda j, i: (i, 0)),
            pl.BlockSpec((None, d, bn), lambda j, i: (layer, 0, j)),
            pl.BlockSpec((None, d, bn), lambda j, i: (layer, 0, nj + j)),
            pl.BlockSpec((CONV_WIDTH, bn), lambda j, i: (0, j)),
            pl.BlockSpec((CONV_WIDTH, bn), lambda j, i: (0, nj + j)),
            pl.BlockSpec((1, bn), lambda j, i: (0, j)),
            pl.BlockSpec((1, bn), lambda j, i: (0, nj + j)),
        ],
        out_specs=pl.BlockSpec((bm, bn), lambda j, i: (i, j)),
        out_shape=jax.ShapeDtypeStruct((m, dff), BF16),
        scratch_shapes=[pltpu.VMEM((SUBLANES, bn), F32), pltpu.VMEM((SUBLANES, bn), F32)],
        compiler_params=_params(("arbitrary", "arbitrary"), vmem),
        name="ffn_up_conv_act",
    )(h2, w_up_all, w_up_all, conv_w, conv_w, conv_b.reshape(1, -1), conv_b.reshape(1, -1))


def _ffn_down_kernel(a_ref, w_ref, x_ref, gpost_ref, gpre_ref, xo_ref, *h_ref):
    rows = a_ref.shape[0] // DOWN_ROW_CHUNKS
    for r in range(DOWN_ROW_CHUNKS):
        sl = slice(r * rows, (r + 1) * rows)
        y = jnp.dot(a_ref[sl, :], w_ref[...], preferred_element_type=F32)
        x_new = x_ref[sl, :] + _rms(y, gpost_ref[...])
        xo_ref[sl, :] = x_new
        if h_ref:
            h_ref[0][sl, :] = _rms(x_new, gpre_ref[...]).astype(h_ref[0].dtype)


def _ffn_down(act, w_down, x2, g_post, g_pre_next):
    m, d = x2.shape
    dff = act.shape[1]
    bm = _pick(m, (512, 256, 128))
    with_h = g_pre_next is not None
    g_pre = g_pre_next if with_h else g_post
    out_specs = [pl.BlockSpec((bm, d), lambda i: (i, 0))]
    out_shape = [jax.ShapeDtypeStruct((m, d), F32)]
    if with_h:
        out_specs.append(pl.BlockSpec((bm, d), lambda i: (i, 0)))
        out_shape.append(jax.ShapeDtypeStruct((m, d), BF16))
    vmem = 2 * dff * d + 2 * bm * (2 * dff + 4 * d + 4 * d + 2 * d) + 6 * bm * d * 4
    outs = pl.pallas_call(
        _ffn_down_kernel,
        grid=(m // bm,),
        in_specs=[
            pl.BlockSpec((bm, dff), lambda i: (i, 0)),
            pl.BlockSpec((dff, d), lambda i: (0, 0), pipeline_mode=pl.Buffered(1)),
            pl.BlockSpec((bm, d), lambda i: (i, 0)),
            pl.BlockSpec((1, d), lambda i: (0, 0)),
            pl.BlockSpec((1, d), lambda i: (0, 0)),
        ],
        out_specs=out_specs,
        out_shape=out_shape,
        compiler_params=_params(("arbitrary",), vmem),
        name="ffn_down_norm",
    )(act, w_down, x2, g_post.reshape(1, d), g_pre.reshape(1, d))
    return (outs[0], outs[1]) if with_h else (outs[0], None)


def kernel(x, rel_bias, w_in, b_f, w_pa, w_pb, w_o, w_up, conv_w, conv_b, w_down,
           g_mix_pre, g_mix_post, g_ffn_pre, g_ffn_post):
    b, t, d = x.shape
    depth = w_in.shape[0]
    m = b * t
    assert t % max(dil for _, dil in A_PATTERNS) == 0 and (t // A_PATTERNS[-1][1]) % Q_BLOCK == 0
    assert w_in.shape[2] == N_QKV + B_HEADS + 2 * d

    bias_tiles = _bias_tiles(rel_bias)
    wt_in = jnp.swapaxes(w_in, 1, 2)
    x2 = x.reshape(m, d)
    h = _pre_norm(x2, g_mix_pre[0])
    for layer in range(depth):
        bf_col = jnp.zeros((LANES, 1), F32).at[:B_HEADS, 0].set(b_f[layer])

        proj = _matmul_wt(h, wt_in, layer, 0, N_QKV, "proj_qkv")
        gates = _matmul_wt(h, wt_in, layer, N_QKV + B_HEADS, 2 * d, "proj_gates")
        c_row, c_col = _forget(h.reshape(b, t, d), wt_in, layer, N_QKV, bf_col)
        proj3 = proj.reshape(b, t, N_QKV)
        y_a = _dilated(proj3, bias_tiles)
        y_b = _fox(proj3, c_row, c_col)
        x2, h2 = _mix(y_a.reshape(m, -1), y_b.reshape(m, -1), gates,
                      w_pa[layer].astype(BF16), w_pb[layer].astype(BF16), w_o[layer].astype(BF16),
                      x2, g_mix_post[layer], g_ffn_pre[layer])
        act = _ffn_up(h2, w_up, layer, conv_w[layer], conv_b[layer], t)
        g_next = g_mix_pre[layer + 1] if layer + 1 < depth else None
        x2, h = _ffn_down(act, w_down[layer].astype(BF16), x2, g_ffn_post[layer], g_next)
    return x2.reshape(b, t, d)
```

```python
import functools
import math

import jax
import jax.numpy as jnp
from jax import lax
from jax.experimental import pallas as pl
from jax.experimental.pallas import tpu as pltpu

HEAD_DIM = 128
A_PATTERNS = ((128, 1), (512, 4), (2048, 16))
A_GROUPS = len(A_PATTERNS)
A_HEADS_PER_GROUP = 4
A_HEADS = A_GROUPS * A_HEADS_PER_GROUP
B_HEADS = 8
N_A = 3 * A_HEADS * HEAD_DIM
N_B = 3 * B_HEADS * HEAD_DIM
N_QKV = N_A + N_B
N_BUCKETS = 32
REL_MAX_DIST = 2048
CONV_WIDTH = 3
Q_BLOCK = 128
EPS = 1e-6
NEG = -1e30
SCALE = HEAD_DIM ** -0.5

LANES = 128
SUBLANES = 8
V7X_VMEM_BYTES = 64 * 1024 * 1024
VMEM_BUDGET_BYTES = V7X_VMEM_BYTES - 8 * 1024 * 1024

F32 = jnp.float32
BF16 = jnp.bfloat16


def _params(semantics, vmem_bytes):
    return pltpu.CompilerParams(
        dimension_semantics=semantics,
        vmem_limit_bytes=int(min(max(vmem_bytes, 16 * 1024 * 1024), VMEM_BUDGET_BYTES)),
    )


def _pick(n, candidates):
    for c in candidates:
        if c <= n and n % c == 0:
            return c
    return n


def _rms(x, g):
    return x * lax.rsqrt(jnp.mean(x * x, axis=-1, keepdims=True) + EPS) * g


def _norm_kernel(x_ref, g_ref, h_ref):
    h_ref[...] = _rms(x_ref[...], g_ref[...]).astype(h_ref.dtype)


def _pre_norm(x2, g):
    m, d = x2.shape
    bm = _pick(m, (512, 256, 128))
    return pl.pallas_call(
        _norm_kernel,
        grid=(m // bm,),
        in_specs=[pl.BlockSpec((bm, d), lambda i: (i, 0)), pl.BlockSpec((1, d), lambda i: (0, 0))],
        out_specs=pl.BlockSpec((bm, d), lambda i: (i, 0)),
        out_shape=jax.ShapeDtypeStruct((m, d), BF16),
        compiler_params=_params(("arbitrary",), 6 * bm * d * 4),
        name="pre_norm",
    )(x2, g.reshape(1, d))


def _mm_nt_kernel(a_ref, wt_ref, o_ref):
    wt = wt_ref[0].astype(BF16)
    o_ref[...] = lax.dot_general(a_ref[...], wt, (((1,), (1,)), ((), ())),
                                 preferred_element_type=F32).astype(o_ref.dtype)


def _matmul_wt(a, wt_all, layer, row0, n, name):
    m, k = a.shape
    assert row0 % SUBLANES == 0
    bm = _pick(m, (1024, 512, 256, 128))
    bn = _pick(n, (2048, 1536, 1280, 1024, 768, 512, 256, 128))
    vmem = 2 * (2 * bm * k + 4 * k * bn + 2 * bm * bn) + 2 * k * bn + 2 * 4 * bm * bn
    wspec = pl.BlockSpec((pl.Element(1), pl.Element(bn), pl.Element(k)),
                         lambda j, i: (layer, pl.multiple_of(row0 + j * bn, SUBLANES), 0))
    return pl.pallas_call(
        _mm_nt_kernel,
        grid=(n // bn, m // bm),
        in_specs=[pl.BlockSpec((bm, k), lambda j, i: (i, 0)), wspec],
        out_specs=pl.BlockSpec((bm, bn), lambda j, i: (i, j)),
        out_shape=jax.ShapeDtypeStruct((m, n), BF16),
        compiler_params=_params(("arbitrary", "arbitrary"), vmem),
        name=name,
    )(a, wt_all)


def _lane_cumsum(x):
    n = x.shape[1]
    lane = lax.broadcasted_iota(jnp.int32, x.shape, 1)
    s = 1
    while s < n:
        x = x + jnp.where(lane >= s, pltpu.roll(x, s, 1), 0.0)
        s *= 2
    return x


def _forget_kernel(h_ref, wft_ref, bf_ref, crow_ref, ccol_ref):
    d = wft_ref.shape[-1]
    wft = jnp.concatenate([wft_ref[0].astype(BF16), jnp.zeros((LANES - B_HEADS, d), BF16)], axis=0)
    ft = lax.dot_general(wft, h_ref[0], (((1,), (1,)), ((), ())), preferred_element_type=F32)
    z = ft + bf_ref[...]
    logf = jnp.minimum(z, 0.0) - jnp.log1p(jnp.exp(-jnp.abs(z)))
    c = _lane_cumsum(logf)
    crow_ref[0] = c[:B_HEADS]
    ccol_ref[0] = c.T


def _forget(h3, wt_all, layer, row0, bf_col):
    b, t, d = h3.shape
    assert row0 % SUBLANES == 0 and B_HEADS == SUBLANES
    return pl.pallas_call(
        _forget_kernel,
        grid=(b,),
        in_specs=[
            pl.BlockSpec((1, t, d), lambda i: (i, 0, 0)),
            pl.BlockSpec((pl.Element(1), pl.Element(B_HEADS), pl.Element(d)), lambda i: (layer, row0, 0)),
            pl.BlockSpec((LANES, 1), lambda i: (0, 0)),
        ],
        out_specs=[
            pl.BlockSpec((1, B_HEADS, t), lambda i: (i, 0, 0)),
            pl.BlockSpec((1, t, LANES), lambda i: (i, 0, 0)),
        ],
        out_shape=[
            jax.ShapeDtypeStruct((b, B_HEADS, t), F32),
            jax.ShapeDtypeStruct((b, t, LANES), F32),
        ],
        compiler_params=_params(("arbitrary",), 2 * 2 * t * d + 16 * t * LANES * 4),
        name="forget_cumsum",
    )(h3, wt_all, bf_col)


def _bias_kernel(tbl_ref, o_ref):
    hd = pl.program_id(0)
    grp = hd // A_HEADS_PER_GROUP
    dil = jnp.where(grp == 0, A_PATTERNS[0][1], jnp.where(grp == 1, A_PATTERNS[1][1], A_PATTERNS[2][1]))
    shape = (Q_BLOCK, 2 * Q_BLOCK)
    qi = lax.broadcasted_iota(jnp.int32, shape, 0)
    kj = lax.broadcasted_iota(jnp.int32, shape, 1)
    delta = qi + Q_BLOCK - kj
    dist = jnp.maximum(delta, 0) * dil
    max_exact = N_BUCKETS // 2
    nf = jnp.maximum(dist, 1).astype(F32)
    large = max_exact + (jnp.log(nf / max_exact) / math.log(REL_MAX_DIST / max_exact)
                         * (N_BUCKETS - max_exact)).astype(jnp.int32)
    large = jnp.minimum(large, N_BUCKETS - 1)
    bucket = jnp.where(dist < max_exact, dist, large)
    bias = jnp.zeros(shape, F32)
    for i in range(N_BUCKETS):
        bias = jnp.where(bucket == i, tbl_ref[i, hd], bias)
    span = Q_BLOCK
    valid = (delta >= 0) & (delta <= span)
    o_ref[0] = jnp.where(valid, bias, NEG)


def _bias_tiles(rel_bias):
    return pl.pallas_call(
        _bias_kernel,
        grid=(A_HEADS,),
        in_specs=[pl.BlockSpec(memory_space=pltpu.SMEM)],
        out_specs=pl.BlockSpec((1, Q_BLOCK, 2 * Q_BLOCK), lambda i: (i, 0, 0)),
        out_shape=jax.ShapeDtypeStruct((A_HEADS, Q_BLOCK, 2 * Q_BLOCK), F32),
        compiler_params=_params(("arbitrary",), 0),
        name="rel_bias_tiles",
    )(rel_bias)


STRIDE = 4


def _dilated_kernel(*refs, t):
    qkv_refs = refs[:9]
    bias_refs = refs[9:12]
    o_ref = refs[12]
    stage, stage2 = refs[13:15]
    qk_bufs = refs[15:19]
    vaugs = refs[19:22]
    o_s = refs[22:25]
    lw_s = refs[25:28]
    o_mid, lw_mid = refs[28:30]
    nblk = t // Q_BLOCK

    for vaug in vaugs:
        vaug[:, HEAD_DIM:] = jnp.ones((t, HEAD_DIM), BF16)

    def deinterleave(src_ref, dil, write):
        stage[...] = src_ref[0].astype(F32)
        quarter = t // STRIDE
        if dil == STRIDE:
            for r in range(STRIDE):
                write(r * quarter, stage[pl.ds(r, quarter, stride=STRIDE), :])
        else:
            for r in range(STRIDE):
                stage2[r * quarter:(r + 1) * quarter, :] = stage[pl.ds(r, quarter, stride=STRIDE), :]
            sub = quarter // STRIDE
            for r in range(STRIDE):
                for j in range(STRIDE):
                    write((r + STRIDE * j) * sub, stage2[pl.ds(r * quarter + j, sub, stride=STRIDE), :])

    for g, (window, dil) in enumerate(A_PATTERNS):
        assert window // dil == Q_BLOCK and dil in (1, STRIDE, STRIDE ** 2)
        q_ref, k_ref, v_ref = qkv_refs[3 * g:3 * g + 3]
        vaug = vaugs[g]
        length = t // dil
        nb = length // Q_BLOCK
        if dil > 1:
            qd, kd = qk_bufs[2 * (g - 1):2 * g]

            def to_q(row0, x, qd=qd):
                qd[row0:row0 + x.shape[0], :] = x.astype(BF16)

            def to_k(row0, x, kd=kd):
                kd[row0:row0 + x.shape[0], :] = x.astype(BF16)

            def to_v(row0, x, vaug=vaug):
                vaug[row0:row0 + x.shape[0], :HEAD_DIM] = x.astype(BF16)

            deinterleave(q_ref, dil, to_q)
            deinterleave(k_ref, dil, to_k)
            deinterleave(v_ref, dil, to_v)
        else:
            vaug[:, :HEAD_DIM] = v_ref[0]
        bias = bias_refs[g][0]
        for blk in range(nblk):
            r, n = divmod(blk, nb)
            lo, hi = blk * Q_BLOCK, (blk + 1) * Q_BLOCK
            klo = lo if n == 0 else lo - Q_BLOCK
            if dil > 1:
                qn, kw = qd[lo:hi, :], kd[klo:hi, :]
            else:
                qn, kw = q_ref[0, lo:hi, :], k_ref[0, klo:hi, :]
            bt = bias[:, Q_BLOCK:] if n == 0 else bias
            s = lax.dot_general(qn, kw, (((1,), (1,)), ((), ())), preferred_element_type=F32) * SCALE + bt
            m = jnp.max(s, axis=-1, keepdims=True)
            p = jnp.exp(s - m)
            acc = jnp.dot(p.astype(BF16), vaug[klo:hi, :], preferred_element_type=F32)
            l = acc[:, HEAD_DIM:]
            o_blk = acc[:, :HEAD_DIM] / l
            lw_blk = m + jnp.log(l)
            if dil == STRIDE ** 2:
                r4, j4 = r % STRIDE, r // STRIDE
                rows = pl.ds(r4 * (t // STRIDE) + j4, Q_BLOCK, stride=STRIDE)
                o_mid[rows, :] = o_blk
                lw_mid[rows, :] = lw_blk
            else:
                rows = pl.ds(n * Q_BLOCK * dil + r, Q_BLOCK, stride=dil) if dil > 1 else pl.ds(lo, Q_BLOCK)
                o_s[g][rows, :] = o_blk
                lw_s[g][rows, :] = lw_blk
        if dil == STRIDE ** 2:
            quarter = t // STRIDE
            for r4 in range(STRIDE):
                o_s[g][pl.ds(r4, quarter, stride=STRIDE), :] = o_mid[r4 * quarter:(r4 + 1) * quarter, :]
                lw_s[g][pl.ds(r4, quarter, stride=STRIDE), :] = lw_mid[r4 * quarter:(r4 + 1) * quarter, :]

    chunk = 256

    def merge(c, carry):
        rows = pl.ds(pl.multiple_of(c * chunk, chunk), chunk)
        lws = [lw_s[g][rows, :] for g in range(A_GROUPS)]
        top = jnp.maximum(jnp.maximum(lws[0], lws[1]), lws[2])
        num = jnp.zeros((chunk, HEAD_DIM), F32)
        den = jnp.zeros((chunk, HEAD_DIM), F32)
        for g in range(A_GROUPS):
            w = jnp.exp(lws[g] - top)
            num = num + w * o_s[g][rows, :]
            den = den + w
        o_ref[0, rows, :] = (num / den).astype(o_ref.dtype)
        return carry

    lax.fori_loop(0, t // chunk, merge, 0)


def _dilated(proj3, bias_tiles):
    b, t, _ = proj3.shape
    hpg = A_HEADS_PER_GROUP

    def col(kind, g):
        return lambda i, h: (i, 0, kind * A_HEADS + g * hpg + h)

    in_specs = []
    for g in range(A_GROUPS):
        for kind in range(3):
            in_specs.append(pl.BlockSpec((1, t, HEAD_DIM), col(kind, g)))
    for g in range(A_GROUPS):
        in_specs.append(pl.BlockSpec((1, Q_BLOCK, 2 * Q_BLOCK), lambda i, h, g=g: (g * hpg + h, 0, 0)))
    plane = t * HEAD_DIM
    assert A_PATTERNS[0][1] == 1 and all(dil > 1 for _, dil in A_PATTERNS[1:])
    scratch = [pltpu.VMEM((t, HEAD_DIM), F32)] * 2 + [pltpu.VMEM((t, HEAD_DIM), BF16)] * 4
    scratch += [pltpu.VMEM((t, 2 * HEAD_DIM), BF16)] * 3 + [pltpu.VMEM((t, HEAD_DIM), F32)] * 8
    vmem = 2 * 10 * plane * 2 + 10 * plane * 4 + 10 * plane * 2 + 16 * 1024 * 1024
    return pl.pallas_call(
        functools.partial(_dilated_kernel, t=t),
        grid=(b, hpg),
        in_specs=in_specs,
        out_specs=pl.BlockSpec((1, t, HEAD_DIM), lambda i, h: (i, 0, h)),
        out_shape=jax.ShapeDtypeStruct((b, t, hpg * HEAD_DIM), BF16),
        scratch_shapes=scratch,
        compiler_params=_params(("arbitrary", "arbitrary"), vmem),
        name="dilated_attention",
    )(*([proj3] * 9), *([bias_tiles] * 3))


LOG2E = math.log2(math.e)
FOX_HEADS_PER_STEP = 4
FOX_TQ = 512
FOX_TK = 512


def _fox_kernel(q_ref, k_ref, v_ref, crow_ref, ccol_ref, o_ref, vaug_ref, *, tq, tk, nh):
    hb = pl.program_id(1)
    qi = pl.program_id(2)
    wide = 2 * HEAD_DIM

    @pl.when(qi == 0)
    def _():
        for i in range(nh):
            vaug_ref[:, i * wide:i * wide + HEAD_DIM] = v_ref[0, :, i * HEAD_DIM:(i + 1) * HEAD_DIM]
            vaug_ref[:, i * wide + HEAD_DIM:(i + 1) * wide] = jnp.ones((vaug_ref.shape[0], HEAD_DIM), BF16)

    lane = lax.broadcasted_iota(jnp.int32, (tq, LANES), 1)
    ccol = ccol_ref[0]
    qs, cts = [], []
    for i in range(nh):
        qs.append(q_ref[0, :, i * HEAD_DIM:(i + 1) * HEAD_DIM])
        c_t = jnp.sum(jnp.where(lane == hb * nh + i, ccol, 0.0), axis=1, keepdims=True)
        cts.append(c_t * LOG2E)

    def tile(j, carry, diagonal):
        start = pl.multiple_of(j * tk, tk)
        out = []
        for i in range(nh):
            m, acc = carry[i]
            k = k_ref[0, pl.ds(start, tk), i * HEAD_DIM:(i + 1) * HEAD_DIM]
            va = vaug_ref[pl.ds(start, tk), i * wide:(i + 1) * wide]
            c_s = crow_ref[0, pl.ds(hb * nh + i, 1), pl.ds(start, tk)] * LOG2E
            z = lax.dot_general(qs[i], k, (((1,), (1,)), ((), ())), preferred_element_type=F32)
            a = z * (SCALE * LOG2E) - c_s
            if diagonal:
                row = lax.broadcasted_iota(jnp.int32, (tq, tk), 0)
                colv = lax.broadcasted_iota(jnp.int32, (tq, tk), 1)
                a = jnp.where(colv - row <= qi * tq - j * tk, a, NEG)
            m_new = jnp.maximum(m, jnp.max(a, axis=-1, keepdims=True) + cts[i])
            alpha = jnp.exp2(m - m_new)
            p = jnp.exp2(a + (cts[i] - m_new))
            acc = alpha * acc + jnp.dot(p.astype(BF16), va, preferred_element_type=F32)
            out.append((m_new, acc))
        return tuple(out)

    init = tuple((jnp.full((tq, 1), NEG, F32), jnp.zeros((tq, wide), F32)) for _ in range(nh))
    n_full = (qi * tq) // tk
    carry = lax.fori_loop(0, n_full, lambda j, c: tile(j, c, False), init)
    carry = tile(n_full, carry, True)
    for i in range(nh):
        _, acc = carry[i]
        o_ref[0, :, i * HEAD_DIM:(i + 1) * HEAD_DIM] = (acc[:, :HEAD_DIM] / acc[:, HEAD_DIM:]).astype(o_ref.dtype)


def _fox(proj3, c_row, c_col):
    b, t, _ = proj3.shape
    nh = FOX_HEADS_PER_STEP
    tq = _pick(t, (FOX_TQ, 128))
    tk = _pick(t, (FOX_TK, 128))
    assert tk % tq == 0
    width = nh * HEAD_DIM
    qoff = N_A // width
    koff = qoff + B_HEADS // nh
    voff = koff + B_HEADS // nh
    assert N_A % width == 0 and B_HEADS % nh == 0
    return pl.pallas_call(
        functools.partial(_fox_kernel, tq=tq, tk=tk, nh=nh),
        grid=(b, B_HEADS // nh, t // tq),
        in_specs=[
            pl.BlockSpec((1, tq, width), lambda i, h, q: (i, q, qoff + h)),
            pl.BlockSpec((1, t, width), lambda i, h, q: (i, 0, koff + h)),
            pl.BlockSpec((1, t, width), lambda i, h, q: (i, 0, voff + h)),
            pl.BlockSpec((1, B_HEADS, t), lambda i, h, q: (i, 0, 0)),
            pl.BlockSpec((1, tq, LANES), lambda i, h, q: (i, q, 0)),
        ],
        out_specs=pl.BlockSpec((1, tq, width), lambda i, h, q: (i, q, h)),
        out_shape=jax.ShapeDtypeStruct((b, t, B_HEADS * HEAD_DIM), BF16),
        scratch_shapes=[pltpu.VMEM((t, 2 * width), BF16)],
        compiler_params=_params(("arbitrary", "arbitrary", "arbitrary"), 8 * t * width * 2 + 16 * 1024 * 1024),
        name="fox_attention",
    )(proj3, proj3, proj3, c_row, c_col)


MIX_ROW_CHUNKS = 4
DOWN_ROW_CHUNKS = 4


def _mix_kernel(ya_ref, yb_ref, ga_ref, gb_ref, wpa_ref, wpb_ref, wo_ref, x_ref, gpost_ref, gpre_ref,
                xo_ref, h_ref):
    rows = ya_ref.shape[0] // MIX_ROW_CHUNKS
    for r in range(MIX_ROW_CHUNKS):
        sl = slice(r * rows, (r + 1) * rows)
        pa = jnp.dot(ya_ref[sl, :], wpa_ref[...], preferred_element_type=F32)
        pb = jnp.dot(yb_ref[sl, :], wpb_ref[...], preferred_element_type=F32)
        ga = jax.nn.sigmoid(ga_ref[sl, :].astype(F32))
        gb = jax.nn.sigmoid(gb_ref[sl, :].astype(F32))
        merged = (ga * pa + gb * pb).astype(BF16)
        mo = jnp.dot(merged, wo_ref[...], preferred_element_type=F32)
        x_new = x_ref[sl, :] + _rms(mo, gpost_ref[...])
        xo_ref[sl, :] = x_new
        h_ref[sl, :] = _rms(x_new, gpre_ref[...]).astype(h_ref.dtype)


def _mix(ya, yb, gates, wpa, wpb, wo, x2, g_post, g_pre):
    m, d = x2.shape
    ka, kb = ya.shape[1], yb.shape[1]
    bm = _pick(m, (512, 256, 128))
    const = lambda i: (0, 0)
    single = dict(pipeline_mode=pl.Buffered(1))
    vmem = 2 * (ka + kb + d) * d + 2 * bm * (2 * (ka + kb + 2 * d) + 4 * d + 4 * d + 2 * d) + 8 * bm * d * 4
    return pl.pallas_call(
        _mix_kernel,
        grid=(m // bm,),
        in_specs=[
            pl.BlockSpec((bm, ka), lambda i: (i, 0)),
            pl.BlockSpec((bm, kb), lambda i: (i, 0)),
            pl.BlockSpec((bm, d), lambda i: (i, 0)),
            pl.BlockSpec((bm, d), lambda i: (i, 1)),
            pl.BlockSpec((ka, d), const, **single),
            pl.BlockSpec((kb, d), const, **single),
            pl.BlockSpec((d, d), const, **single),
            pl.BlockSpec((bm, d), lambda i: (i, 0)),
            pl.BlockSpec((1, d), const),
            pl.BlockSpec((1, d), const),
        ],
        out_specs=[pl.BlockSpec((bm, d), lambda i: (i, 0)), pl.BlockSpec((bm, d), lambda i: (i, 0))],
        out_shape=[jax.ShapeDtypeStruct((m, d), F32), jax.ShapeDtypeStruct((m, d), BF16)],
        compiler_params=_params(("arbitrary",), vmem),
        name="gated_mix",
    )(ya, yb, gates, gates, wpa, wpb, wo, x2, g_post.reshape(1, d), g_pre.reshape(1, d))


def _ffn_up_kernel(h_ref, wg_ref, wv_ref, cwg_ref, cwv_ref, cbg_ref, cbv_ref, o_ref, carry_g, carry_v,
                   *, blocks_per_seq):
    i = pl.program_id(1)
    bm = h_ref.shape[0]

    @pl.when(i % blocks_per_seq == 0)
    def _():
        carry_g[...] = jnp.zeros_like(carry_g)
        carry_v[...] = jnp.zeros_like(carry_v)

    def conv3(u, carry_ref, cw, cb):
        ext = jnp.concatenate([carry_ref[...], u], axis=0)
        x1 = pltpu.roll(ext, 1, 0)[SUBLANES:]
        x2 = pltpu.roll(ext, 2, 0)[SUBLANES:]
        carry_ref[...] = u[bm - SUBLANES:]
        return cw[0:1] * x2 + cw[1:2] * x1 + cw[2:3] * u + cb

    h = h_ref[...]
    ug = jnp.dot(h, wg_ref[...].astype(BF16), preferred_element_type=F32)
    uv = jnp.dot(h, wv_ref[...].astype(BF16), preferred_element_type=F32)
    gate = conv3(ug, carry_g, cwg_ref[...], cbg_ref[...])
    val = conv3(uv, carry_v, cwv_ref[...], cbv_ref[...])
    cdf = 0.5 * (1.0 + jnp.tanh(math.sqrt(2.0 / math.pi) * (gate + 0.044715 * (gate * gate * gate))))
    o_ref[...] = (gate * cdf * val).astype(o_ref.dtype)


def _ffn_up(h2, w_up_all, layer, conv_w, conv_b, t):
    m, d = h2.shape
    dff = w_up_all.shape[2] // 2
    bn = _pick(dff, (512, 256, 128))
    nj = dff // bn
    bm = _pick(t, (1024, 512, 256, 128))
    vmem = 2 * (2 * bm * d + 2 * 4 * d * bn + 2 * bm * bn) + 2 * 2 * d * bn + 32 * bm * bn
    return pl.pallas_call(
        functools.partial(_ffn_up_kernel, blocks_per_seq=t // bm),
        grid=(nj, m // bm),
        in_specs=[
            pl.BlockSpec((bm, d), lambda j, i: (i, 0)),
            pl.BlockSpec((None, d, bn), lambda j, i: (layer, 0, j)),
            pl.BlockSpec((None, d, bn), lambda j, i: (layer, 0, nj + j)),
            pl.BlockSpec((CONV_WIDTH, bn), lambda j, i: (0, j)),
            pl.BlockSpec((CONV_WIDTH, bn), lambda j, i: (0, nj + j)),
            pl.BlockSpec((1, bn), lambda j, i: (0, j)),
            pl.BlockSpec((1, bn), lambda j, i: (0, nj + j)),
        ],
        out_specs=pl.BlockSpec((bm, bn), lambda j, i: (i, j)),
        out_shape=jax.ShapeDtypeStruct((m, dff), BF16),
        scratch_shapes=[pltpu.VMEM((SUBLANES, bn), F32), pltpu.VMEM((SUBLANES, bn), F32)],
        compiler_params=_params(("arbitrary", "arbitrary"), vmem),
        name="ffn_up_conv_act",
    )(h2, w_up_all, w_up_all, conv_w, conv_w, conv_b.reshape(1, -1), conv_b.reshape(1, -1))


def _ffn_down_kernel(a_ref, w_ref, x_ref, gpost_ref, gpre_ref, xo_ref, *h_ref):
    rows = a_ref.shape[0] // DOWN_ROW_CHUNKS
    for r in range(DOWN_ROW_CHUNKS):
        sl = slice(r * rows, (r + 1) * rows)
        y = jnp.dot(a_ref[sl, :], w_ref[...], preferred_element_type=F32)
        x_new = x_ref[sl, :] + _rms(y, gpost_ref[...])
        xo_ref[sl, :] = x_new
        if h_ref:
            h_ref[0][sl, :] = _rms(x_new, gpre_ref[...]).astype(h_ref[0].dtype)


def _ffn_down(act, w_down, x2, g_post, g_pre_next):
    m, d = x2.shape
    dff = act.shape[1]
    bm = _pick(m, (512, 256, 128))
    with_h = g_pre_next is not None
    g_pre = g_pre_next if with_h else g_post
    out_specs = [pl.BlockSpec((bm, d), lambda i: (i, 0))]
    out_shape = [jax.ShapeDtypeStruct((m, d), F32)]
    if with_h:
        out_specs.append(pl.BlockSpec((bm, d), lambda i: (i, 0)))
        out_shape.append(jax.ShapeDtypeStruct((m, d), BF16))
    vmem = 2 * dff * d + 2 * bm * (2 * dff + 4 * d + 4 * d + 2 * d) + 6 * bm * d * 4
    outs = pl.pallas_call(
        _ffn_down_kernel,
        grid=(m // bm,),
        in_specs=[
            pl.BlockSpec((bm, dff), lambda i: (i, 0)),
            pl.BlockSpec((dff, d), lambda i: (0, 0), pipeline_mode=pl.Buffered(1)),
            pl.BlockSpec((bm, d), lambda i: (i, 0)),
            pl.BlockSpec((1, d), lambda i: (0, 0)),
            pl.BlockSpec((1, d), lambda i: (0, 0)),
        ],
        out_specs=out_specs,
        out_shape=out_shape,
        compiler_params=_params(("arbitrary",), vmem),
        name="ffn_down_norm",
    )(act, w_down, x2, g_post.reshape(1, d), g_pre.reshape(1, d))
    return (outs[0], outs[1]) if with_h else (outs[0], None)


def kernel(x, rel_bias, w_in, b_f, w_pa, w_pb, w_o, w_up, conv_w, conv_b, w_down,
           g_mix_pre, g_mix_post, g_ffn_pre, g_ffn_post):
    b, t, d = x.shape
    depth = w_in.shape[0]
    m = b * t
    assert t % max(dil for _, dil in A_PATTERNS) == 0 and (t // A_PATTERNS[-1][1]) % Q_BLOCK == 0
    assert w_in.shape[2] == N_QKV + B_HEADS + 2 * d

    bias_tiles = _bias_tiles(rel_bias)
    wt_in = jnp.swapaxes(w_in, 1, 2)
    x2 = x.reshape(m, d)
    h = _pre_norm(x2, g_mix_pre[0])
    for layer in range(depth):
        bf_col = jnp.zeros((LANES, 1), F32).at[:B_HEADS, 0].set(b_f[layer])

        proj = _matmul_wt(h, wt_in, layer, 0, N_QKV, "proj_qkv")
        gates = _matmul_wt(h, wt_in, layer, N_QKV + B_HEADS, 2 * d, "proj_gates")
        c_row, c_col = _forget(h.reshape(b, t, d), wt_in, layer, N_QKV, bf_col)
        proj3 = proj.reshape(b, t, N_QKV)
        y_a = _dilated(proj3, bias_tiles)
        y_b = _fox(proj3, c_row, c_col)
        x2, h2 = _mix(y_a.reshape(m, -1), y_b.reshape(m, -1), gates,
                      w_pa[layer].astype(BF16), w_pb[layer].astype(BF16), w_o[layer].astype(BF16),
                      x2, g_mix_post[layer], g_ffn_pre[layer])
        act = _ffn_up(h2, w_up, layer, conv_w[layer], conv_b[layer], t)
        g_next = g_mix_pre[layer + 1] if layer + 1 < depth else None
        x2, h = _ffn_down(act, w_down[layer].astype(BF16), x2, g_ffn_post[layer], g_next)
    return x2.reshape(b, t, d)
```

```python
import functools
import math

import jax
import jax.numpy as jnp
from jax import lax
from jax.experimental import pallas as pl
from jax.experimental.pallas import tpu as pltpu

HEAD_DIM = 128
A_PATTERNS = ((128, 1), (512, 4), (2048, 16))
A_GROUPS = len(A_PATTERNS)
A_HEADS_PER_GROUP = 4
A_HEADS = A_GROUPS * A_HEADS_PER_GROUP
B_HEADS = 8
N_A = 3 * A_HEADS * HEAD_DIM
N_B = 3 * B_HEADS * HEAD_DIM
N_QKV = N_A + N_B
N_BUCKETS = 32
REL_MAX_DIST = 2048
CONV_WIDTH = 3
Q_BLOCK = 128
EPS = 1e-6
NEG = -1e30
SCALE = HEAD_DIM ** -0.5

LANES = 128
SUBLANES = 8
V7X_VMEM_BYTES = 64 * 1024 * 1024
VMEM_BUDGET_BYTES = V7X_VMEM_BYTES - 8 * 1024 * 1024

F32 = jnp.float32
BF16 = jnp.bfloat16


def _params(semantics, vmem_bytes):
    return pltpu.CompilerParams(
        dimension_semantics=semantics,
        vmem_limit_bytes=int(min(max(vmem_bytes, 16 * 1024 * 1024), VMEM_BUDGET_BYTES)),
    )


def _pick(n, candidates):
    for c in candidates:
        if c <= n and n % c == 0:
            return c
    return n


def _rms(x, g):
    return x * lax.rsqrt(jnp.mean(x * x, axis=-1, keepdims=True) + EPS) * g


def _norm_kernel(x_ref, g_ref, h_ref):
    h_ref[...] = _rms(x_ref[...], g_ref[...]).astype(h_ref.dtype)


def _pre_norm(x2, g):
    m, d = x2.shape
    bm = _pick(m, (512, 256, 128))
    return pl.pallas_call(
        _norm_kernel,
        grid=(m // bm,),
        in_specs=[pl.BlockSpec((bm, d), lambda i: (i, 0)), pl.BlockSpec((1, d), lambda i: (0, 0))],
        out_specs=pl.BlockSpec((bm, d), lambda i: (i, 0)),
        out_shape=jax.ShapeDtypeStruct((m, d), BF16),
        compiler_params=_params(("arbitrary",), 6 * bm * d * 4),
        name="pre_norm",
    )(x2, g.reshape(1, d))


def _mm_nt_kernel(a_ref, wt_ref, o_ref):
    wt = wt_ref[0].astype(BF16)
    o_ref[...] = lax.dot_general(a_ref[...], wt, (((1,), (1,)), ((), ())),
                                 preferred_element_type=F32).astype(o_ref.dtype)


def _matmul_wt(a, wt_all, layer, row0, n, name):
    m, k = a.shape
    assert row0 % SUBLANES == 0
    bm = _pick(m, (1024, 512, 256, 128))
    bn = _pick(n, (2048, 1536, 1280, 1024, 768, 512, 256, 128))
    vmem = 2 * (2 * bm * k + 4 * k * bn + 2 * bm * bn) + 2 * k * bn + 2 * 4 * bm * bn
    wspec = pl.BlockSpec((pl.Element(1), pl.Element(bn), pl.Element(k)),
                         lambda j, i: (layer, pl.multiple_of(row0 + j * bn, SUBLANES), 0))
    return pl.pallas_call(
        _mm_nt_kernel,
        grid=(n // bn, m // bm),
        in_specs=[pl.BlockSpec((bm, k), lambda j, i: (i, 0)), wspec],
        out_specs=pl.BlockSpec((bm, bn), lambda j, i: (i, j)),
        out_shape=jax.ShapeDtypeStruct((m, n), BF16),
        compiler_params=_params(("arbitrary", "arbitrary"), vmem),
        name=name,
    )(a, wt_all)


def _lane_cumsum(x):
    n = x.shape[1]
    lane = lax.broadcasted_iota(jnp.int32, x.shape, 1)
    s = 1
    while s < n:
        x = x + jnp.where(lane >= s, pltpu.roll(x, s, 1), 0.0)
        s *= 2
    return x


def _forget_kernel(h_ref, wft_ref, bf_ref, crow_ref, ccol_ref):
    d = wft_ref.shape[-1]
    wft = jnp.concatenate([wft_ref[0].astype(BF16), jnp.zeros((LANES - B_HEADS, d), BF16)], axis=0)
    ft = lax.dot_general(wft, h_ref[0], (((1,), (1,)), ((), ())), preferred_element_type=F32)
    z = ft[:B_HEADS] + bf_ref[:B_HEADS, :]
    logf = jnp.minimum(z, 0.0) - jnp.log1p(jnp.exp(-jnp.abs(z)))
    c = _lane_cumsum(logf)
    crow_ref[0] = c
    ccol_ref[0] = jnp.concatenate([c, jnp.zeros((LANES - B_HEADS, c.shape[1]), F32)], axis=0).T


def _forget(h3, wt_all, layer, row0, bf_col):
    b, t, d = h3.shape
    assert row0 % SUBLANES == 0 and B_HEADS == SUBLANES
    return pl.pallas_call(
        _forget_kernel,
        grid=(b,),
        in_specs=[
            pl.BlockSpec((1, t, d), lambda i: (i, 0, 0)),
            pl.BlockSpec((pl.Element(1), pl.Element(B_HEADS), pl.Element(d)), lambda i: (layer, row0, 0)),
            pl.BlockSpec((LANES, 1), lambda i: (0, 0)),
        ],
        out_specs=[
            pl.BlockSpec((1, B_HEADS, t), lambda i: (i, 0, 0)),
            pl.BlockSpec((1, t, LANES), lambda i: (i, 0, 0)),
        ],
        out_shape=[
            jax.ShapeDtypeStruct((b, B_HEADS, t), F32),
            jax.ShapeDtypeStruct((b, t, LANES), F32),
        ],
        compiler_params=_params(("arbitrary",), 2 * 2 * t * d + 16 * t * LANES * 4),
        name="forget_cumsum",
    )(h3, wt_all, bf_col)


def _bias_kernel(tbl_ref, o_ref):
    hd = pl.program_id(0)
    grp = hd // A_HEADS_PER_GROUP
    dil = jnp.where(grp == 0, A_PATTERNS[0][1], jnp.where(grp == 1, A_PATTERNS[1][1], A_PATTERNS[2][1]))
    shape = (Q_BLOCK, 2 * Q_BLOCK)
    qi = lax.broadcasted_iota(jnp.int32, shape, 0)
    kj = lax.broadcasted_iota(jnp.int32, shape, 1)
    delta = qi + Q_BLOCK - kj
    dist = jnp.maximum(delta, 0) * dil
    max_exact = N_BUCKETS // 2
    nf = jnp.maximum(dist, 1).astype(F32)
    large = max_exact + (jnp.log(nf / max_exact) / math.log(REL_MAX_DIST / max_exact)
                         * (N_BUCKETS - max_exact)).astype(jnp.int32)
    large = jnp.minimum(large, N_BUCKETS - 1)
    bucket = jnp.where(dist < max_exact, dist, large)
    bias = jnp.zeros(shape, F32)
    for i in range(N_BUCKETS):
        bias = jnp.where(bucket == i, tbl_ref[i, hd], bias)
    span = Q_BLOCK
    valid = (delta >= 0) & (delta <= span)
    o_ref[0] = jnp.where(valid, bias, NEG)


def _bias_tiles(rel_bias):
    return pl.pallas_call(
        _bias_kernel,
        grid=(A_HEADS,),
        in_specs=[pl.BlockSpec(memory_space=pltpu.SMEM)],
        out_specs=pl.BlockSpec((1, Q_BLOCK, 2 * Q_BLOCK), lambda i: (i, 0, 0)),
        out_shape=jax.ShapeDtypeStruct((A_HEADS, Q_BLOCK, 2 * Q_BLOCK), F32),
        compiler_params=_params(("arbitrary",), 0),
        name="rel_bias_tiles",
    )(rel_bias)


STRIDE = 4


def _dilated_kernel(*refs, t):
    qkv_refs = refs[:9]
    bias_refs = refs[9:12]
    o_ref = refs[12]
    stage, stage2 = refs[13:15]
    qk_bufs = refs[15:19]
    vaugs = refs[19:22]
    o_s = refs[22:25]
    lw_s = refs[25:28]
    o_mid, lw_mid = refs[28:30]
    nblk = t // Q_BLOCK

    for vaug in vaugs:
        vaug[:, HEAD_DIM:] = jnp.ones((t, HEAD_DIM), BF16)

    def deinterleave(src_ref, dil, write):
        stage[...] = src_ref[0].astype(F32)
        quarter = t // STRIDE
        if dil == STRIDE:
            for r in range(STRIDE):
                write(r * quarter, stage[pl.ds(r, quarter, stride=STRIDE), :])
        else:
            for r in range(STRIDE):
                stage2[r * quarter:(r + 1) * quarter, :] = stage[pl.ds(r, quarter, stride=STRIDE), :]
            sub = quarter // STRIDE
            for r in range(STRIDE):
                for j in range(STRIDE):
                    write((r + STRIDE * j) * sub, stage2[pl.ds(r * quarter + j, sub, stride=STRIDE), :])

    for g, (window, dil) in enumerate(A_PATTERNS):
        assert window // dil == Q_BLOCK and dil in (1, STRIDE, STRIDE ** 2)
        q_ref, k_ref, v_ref = qkv_refs[3 * g:3 * g + 3]
        vaug = vaugs[g]
        length = t // dil
        nb = length // Q_BLOCK
        if dil > 1:
            qd, kd = qk_bufs[2 * (g - 1):2 * g]

            def to_q(row0, x, qd=qd):
                qd[row0:row0 + x.shape[0], :] = x.astype(BF16)

            def to_k(row0, x, kd=kd):
                kd[row0:row0 + x.shape[0], :] = x.astype(BF16)

            def to_v(row0, x, vaug=vaug):
                vaug[row0:row0 + x.shape[0], :HEAD_DIM] = x.astype(BF16)

            deinterleave(q_ref, dil, to_q)
            deinterleave(k_ref, dil, to_k)
            deinterleave(v_ref, dil, to_v)
        else:
            vaug[:, :HEAD_DIM] = v_ref[0]
        bias = bias_refs[g][0]
        for blk in range(nblk):
            r, n = divmod(blk, nb)
            lo, hi = blk * Q_BLOCK, (blk + 1) * Q_BLOCK
            klo = lo if n == 0 else lo - Q_BLOCK
            if dil > 1:
                qn, kw = qd[lo:hi, :], kd[klo:hi, :]
            else:
                qn, kw = q_ref[0, lo:hi, :], k_ref[0, klo:hi, :]
            bt = bias[:, Q_BLOCK:] if n == 0 else bias
            s = lax.dot_general(qn, kw, (((1,), (1,)), ((), ())), preferred_element_type=F32) * SCALE + bt
            m = jnp.max(s, axis=-1, keepdims=True)
            p = jnp.exp(s - m)
            acc = jnp.dot(p.astype(BF16), vaug[klo:hi, :], preferred_element_type=F32)
            l = acc[:, HEAD_DIM:]
            o_blk = acc[:, :HEAD_DIM] / l
            lw_blk = m + jnp.log(l)
            if dil == STRIDE ** 2:
                r4, j4 = r % STRIDE, r // STRIDE
                rows = pl.ds(r4 * (t // STRIDE) + j4, Q_BLOCK, stride=STRIDE)
                o_mid[rows, :] = o_blk
                lw_mid[rows, :] = lw_blk
            else:
                rows = pl.ds(n * Q_BLOCK * dil + r, Q_BLOCK, stride=dil) if dil > 1 else pl.ds(lo, Q_BLOCK)
                o_s[g][rows, :] = o_blk
                lw_s[g][rows, :] = lw_blk
        if dil == STRIDE ** 2:
            quarter = t // STRIDE
            for r4 in range(STRIDE):
                o_s[g][pl.ds(r4, quarter, stride=STRIDE), :] = o_mid[r4 * quarter:(r4 + 1) * quarter, :]
                lw_s[g][pl.ds(r4, quarter, stride=STRIDE), :] = lw_mid[r4 * quarter:(r4 + 1) * quarter, :]

    chunk = 256

    def merge(c, carry):
        rows = pl.ds(pl.multiple_of(c * chunk, chunk), chunk)
        lws = [lw_s[g][rows, :] for g in range(A_GROUPS)]
        top = jnp.maximum(jnp.maximum(lws[0], lws[1]), lws[2])
        num = jnp.zeros((chunk, HEAD_DIM), F32)
        den = jnp.zeros((chunk, HEAD_DIM), F32)
        for g in range(A_GROUPS):
            w = jnp.exp(lws[g] - top)
            num = num + w * o_s[g][rows, :]
            den = den + w
        o_ref[0, rows, :] = (num / den).astype(o_ref.dtype)
        return carry

    lax.fori_loop(0, t // chunk, merge, 0)


def _dilated(proj3, bias_tiles):
    b, t, _ = proj3.shape
    hpg = A_HEADS_PER_GROUP

    def col(kind, g):
        return lambda i, h: (i, 0, kind * A_HEADS + g * hpg + h)

    in_specs = []
    for g in range(A_GROUPS):
        for kind in range(3):
            in_specs.append(pl.BlockSpec((1, t, HEAD_DIM), col(kind, g)))
    for g in range(A_GROUPS):
        in_specs.append(pl.BlockSpec((1, Q_BLOCK, 2 * Q_BLOCK), lambda i, h, g=g: (g * hpg + h, 0, 0)))
    plane = t * HEAD_DIM
    assert A_PATTERNS[0][1] == 1 and all(dil > 1 for _, dil in A_PATTERNS[1:])
    scratch = [pltpu.VMEM((t, HEAD_DIM), F32)] * 2 + [pltpu.VMEM((t, HEAD_DIM), BF16)] * 4
    scratch += [pltpu.VMEM((t, 2 * HEAD_DIM), BF16)] * 3 + [pltpu.VMEM((t, HEAD_DIM), F32)] * 8
    vmem = 2 * 10 * plane * 2 + 10 * plane * 4 + 10 * plane * 2 + 16 * 1024 * 1024
    return pl.pallas_call(
        functools.partial(_dilated_kernel, t=t),
        grid=(b, hpg),
        in_specs=in_specs,
        out_specs=pl.BlockSpec((1, t, HEAD_DIM), lambda i, h: (i, 0, h)),
        out_shape=jax.ShapeDtypeStruct((b, t, hpg * HEAD_DIM), BF16),
        scratch_shapes=scratch,
        compiler_params=_params(("arbitrary", "arbitrary"), vmem),
        name="dilated_attention",
    )(*([proj3] * 9), *([bias_tiles] * 3))


LOG2E = math.log2(math.e)
FOX_HEADS_PER_STEP = 4
FOX_TQ = 512
FOX_TK = 512


def _fox_kernel(q_ref, k_ref, v_ref, crow_ref, ccol_ref, o_ref, vaug_ref, *, tq, tk, nh):
    hb = pl.program_id(1)
    qi = pl.program_id(2)
    wide = 2 * HEAD_DIM

    @pl.when(qi == 0)
    def _():
        for i in range(nh):
            vaug_ref[:, i * wide:i * wide + HEAD_DIM] = v_ref[0, :, i * HEAD_DIM:(i + 1) * HEAD_DIM]
            vaug_ref[:, i * wide + HEAD_DIM:(i + 1) * wide] = jnp.ones((vaug_ref.shape[0], HEAD_DIM), BF16)

    lane = lax.broadcasted_iota(jnp.int32, (tq, LANES), 1)
    ccol = ccol_ref[0]
    qs, cts = [], []
    for i in range(nh):
        qs.append(q_ref[0, :, i * HEAD_DIM:(i + 1) * HEAD_DIM])
        c_t = jnp.sum(jnp.where(lane == hb * nh + i, ccol, 0.0), axis=1, keepdims=True)
        cts.append(c_t * LOG2E)

    def tile(j, carry, diagonal):
        start = pl.multiple_of(j * tk, tk)
        out = []
        for i in range(nh):
            m, acc = carry[i]
            k = k_ref[0, pl.ds(start, tk), i * HEAD_DIM:(i + 1) * HEAD_DIM]
            va = vaug_ref[pl.ds(start, tk), i * wide:(i + 1) * wide]
            c_s = crow_ref[0, pl.ds(hb * nh + i, 1), pl.ds(start, tk)] * LOG2E
            z = lax.dot_general(qs[i], k, (((1,), (1,)), ((), ())), preferred_element_type=F32)
            a = z * (SCALE * LOG2E) - c_s
            if diagonal:
                row = lax.broadcasted_iota(jnp.int32, (tq, tk), 0)
                colv = lax.broadcasted_iota(jnp.int32, (tq, tk), 1)
                a = jnp.where(colv - row <= qi * tq - j * tk, a, NEG)
            m_new = jnp.maximum(m, jnp.max(a, axis=-1, keepdims=True) + cts[i])
            alpha = jnp.exp2(m - m_new)
            p = jnp.exp2(a + (cts[i] - m_new))
            acc = alpha * acc + jnp.dot(p.astype(BF16), va, preferred_element_type=F32)
            out.append((m_new, acc))
        return tuple(out)

    init = tuple((jnp.full((tq, 1), NEG, F32), jnp.zeros((tq, wide), F32)) for _ in range(nh))
    n_full = (qi * tq) // tk
    carry = lax.fori_loop(0, n_full, lambda j, c: tile(j, c, False), init)
    carry = tile(n_full, carry, True)
    for i in range(nh):
        _, acc = carry[i]
        o_ref[0, :, i * HEAD_DIM:(i + 1) * HEAD_DIM] = (acc[:, :HEAD_DIM] / acc[:, HEAD_DIM:]).astype(o_ref.dtype)


def _fox(proj3, c_row, c_col):
    b, t, _ = proj3.shape
    nh = FOX_HEADS_PER_STEP
    tq = _pick(t, (FOX_TQ, 128))
    tk = _pick(t, (FOX_TK, 128))
    assert tk % tq == 0
    width = nh * HEAD_DIM
    qoff = N_A // width
    koff = qoff + B_HEADS // nh
    voff = koff + B_HEADS // nh
    assert N_A % width == 0 and B_HEADS % nh == 0
    return pl.pallas_call(
        functools.partial(_fox_kernel, tq=tq, tk=tk, nh=nh),
        grid=(b, B_HEADS // nh, t // tq),
        in_specs=[
            pl.BlockSpec((1, tq, width), lambda i, h, q: (i, q, qoff + h)),
            pl.BlockSpec((1, t, width), lambda i, h, q: (i, 0, koff + h)),
            pl.BlockSpec((1, t, width), lambda i, h, q: (i, 0, voff + h)),
            pl.BlockSpec((1, B_HEADS, t), lambda i, h, q: (i, 0, 0)),
            pl.BlockSpec((1, tq, LANES), lambda i, h, q: (i, q, 0)),
        ],
        out_specs=pl.BlockSpec((1, tq, width), lambda i, h, q: (i, q, h)),
        out_shape=jax.ShapeDtypeStruct((b, t, B_HEADS * HEAD_DIM), BF16),
        scratch_shapes=[pltpu.VMEM((t, 2 * width), BF16)],
        compiler_params=_params(("arbitrary", "arbitrary", "arbitrary"), 8 * t * width * 2 + 16 * 1024 * 1024),
        name="fox_attention",
    )(proj3, proj3, proj3, c_row, c_col)


MIX_ROW_CHUNKS = 4
DOWN_ROW_CHUNKS = 4


def _mix_kernel(ya_ref, yb_ref, ga_ref, gb_ref, wpa_ref, wpb_ref, wo_ref, x_ref, gpost_ref, gpre_ref,
                xo_ref, h_ref):
    rows = ya_ref.shape[0] // MIX_ROW_CHUNKS
    for r in range(MIX_ROW_CHUNKS):
        sl = slice(r * rows, (r + 1) * rows)
        pa = jnp.dot(ya_ref[sl, :], wpa_ref[...], preferred_element_type=F32)
        pb = jnp.dot(yb_ref[sl, :], wpb_ref[...], preferred_element_type=F32)
        ga = jax.nn.sigmoid(ga_ref[sl, :].astype(F32))
        gb = jax.nn.sigmoid(gb_ref[sl, :].astype(F32))
        merged = (ga * pa + gb * pb).astype(BF16)
        mo = jnp.dot(merged, wo_ref[...], preferred_element_type=F32)
        x_new = x_ref[sl, :] + _rms(mo, gpost_ref[...])
        xo_ref[sl, :] = x_new
        h_ref[sl, :] = _rms(x_new, gpre_ref[...]).astype(h_ref.dtype)


def _mix(ya, yb, gates, wpa, wpb, wo, x2, g_post, g_pre):
    m, d = x2.shape
    ka, kb = ya.shape[1], yb.shape[1]
    bm = _pick(m, (512, 256, 128))
    const = lambda i: (0, 0)
    single = dict(pipeline_mode=pl.Buffered(1))
    vmem = 2 * (ka + kb + d) * d + 2 * bm * (2 * (ka + kb + 2 * d) + 4 * d + 4 * d + 2 * d) + 8 * bm * d * 4
    return pl.pallas_call(
        _mix_kernel,
        grid=(m // bm,),
        in_specs=[
            pl.BlockSpec((bm, ka), lambda i: (i, 0)),
            pl.BlockSpec((bm, kb), lambda i: (i, 0)),
            pl.BlockSpec((bm, d), lambda i: (i, 0)),
            pl.BlockSpec((bm, d), lambda i: (i, 1)),
            pl.BlockSpec((ka, d), const, **single),
            pl.BlockSpec((kb, d), const, **single),
            pl.BlockSpec((d, d), const, **single),
            pl.BlockSpec((bm, d), lambda i: (i, 0)),
            pl.BlockSpec((1, d), const),
            pl.BlockSpec((1, d), const),
        ],
        out_specs=[pl.BlockSpec((bm, d), lambda i: (i, 0)), pl.BlockSpec((bm, d), lambda i: (i, 0))],
        out_shape=[jax.ShapeDtypeStruct((m, d), F32), jax.ShapeDtypeStruct((m, d), BF16)],
        compiler_params=_params(("arbitrary",), vmem),
        name="gated_mix",
    )(ya, yb, gates, gates, wpa, wpb, wo, x2, g_post.reshape(1, d), g_pre.reshape(1, d))


def _ffn_up_kernel(h_ref, wg_ref, wv_ref, cwg_ref, cwv_ref, cbg_ref, cbv_ref, o_ref, carry_g, carry_v,
                   *, blocks_per_seq):
    i = pl.program_id(1)
    bm = h_ref.shape[0]

    @pl.when(i % blocks_per_seq == 0)
    def _():
        carry_g[...] = jnp.zeros_like(carry_g)
        carry_v[...] = jnp.zeros_like(carry_v)

    def conv3(u, carry_ref, cw, cb):
        ext = jnp.concatenate([carry_ref[...], u], axis=0)
        x1 = pltpu.roll(ext, 1, 0)[SUBLANES:]
        x2 = pltpu.roll(ext, 2, 0)[SUBLANES:]
        carry_ref[...] = u[bm - SUBLANES:]
        return cw[0:1] * x2 + cw[1:2] * x1 + cw[2:3] * u + cb

    h = h_ref[...]
    ug = jnp.dot(h, wg_ref[...].astype(BF16), preferred_element_type=F32)
    uv = jnp.dot(h, wv_ref[...].astype(BF16), preferred_element_type=F32)
    gate = conv3(ug, carry_g, cwg_ref[...], cbg_ref[...])
    val = conv3(uv, carry_v, cwv_ref[...], cbv_ref[...])
    cdf = 0.5 * (1.0 + jnp.tanh(math.sqrt(2.0 / math.pi) * (gate + 0.044715 * (gate * gate * gate))))
    o_ref[...] = (gate * cdf * val).astype(o_ref.dtype)


def _ffn_up(h2, w_up_all, layer, conv_w, conv_b, t):
    m, d = h2.shape
    dff = w_up_all.shape[2] // 2
    bn = _pick(dff, (512, 256, 128))
    nj = dff // bn
    bm = _pick(t, (1024, 512, 256, 128))
    vmem = 2 * (2 * bm * d + 2 * 4 * d * bn + 2 * bm * bn) + 2 * 2 * d * bn + 32 * bm * bn
    return pl.pallas_call(
        functools.partial(_ffn_up_kernel, blocks_per_seq=t // bm),
        grid=(nj, m // bm),
        in_specs=[
            pl.BlockSpec((bm, d), lambda j, i: (i, 0)),
            pl.BlockSpec((None, d, bn), lambda j, i: (layer, 0, j)),
            pl.BlockSpec((None, d, bn), lambda j, i: (layer, 0, nj + j)),
            pl.BlockSpec((CONV_WIDTH, bn), lambda j, i: (0, j)),
            pl.BlockSpec((CONV_WIDTH, bn), lambda j, i: (0, nj + j)),
            pl.BlockSpec((1, bn), lambda j, i: (0, j)),
            pl.BlockSpec((1, bn), lambda j, i: (0, nj + j)),
        ],
        out_specs=pl.BlockSpec((bm, bn), lambda j, i: (i, j)),
        out_shape=jax.ShapeDtypeStruct((m, dff), BF16),
        scratch_shapes=[pltpu.VMEM((SUBLANES, bn), F32), pltpu.VMEM((SUBLANES, bn), F32)],
        compiler_params=_params(("arbitrary", "arbitrary"), vmem),
        name="ffn_up_conv_act",
    )(h2, w_up_all, w_up_all, conv_w, conv_w, conv_b.reshape(1, -1), conv_b.reshape(1, -1))


def _ffn_down_kernel(a_ref, w_ref, x_ref, gpost_ref, gpre_ref, xo_ref, *h_ref):
    rows = a_ref.shape[0] // DOWN_ROW_CHUNKS
    for r in range(DOWN_ROW_CHUNKS):
        sl = slice(r * rows, (r + 1) * rows)
        y = jnp.dot(a_ref[sl, :], w_ref[...], preferred_element_type=F32)
        x_new = x_ref[sl, :] + _rms(y, gpost_ref[...])
        xo_ref[sl, :] = x_new
        if h_ref:
            h_ref[0][sl, :] = _rms(x_new, gpre_ref[...]).astype(h_ref[0].dtype)


def _ffn_down(act, w_down, x2, g_post, g_pre_next):
    m, d = x2.shape
    dff = act.shape[1]
    bm = _pick(m, (512, 256, 128))
    with_h = g_pre_next is not None
    g_pre = g_pre_next if with_h else g_post
    out_specs = [pl.BlockSpec((bm, d), lambda i: (i, 0))]
    out_shape = [jax.ShapeDtypeStruct((m, d), F32)]
    if with_h:
        out_specs.append(pl.BlockSpec((bm, d), lambda i: (i, 0)))
        out_shape.append(jax.ShapeDtypeStruct((m, d), BF16))
    vmem = 2 * dff * d + 2 * bm * (2 * dff + 4 * d + 4 * d + 2 * d) + 6 * bm * d * 4
    outs = pl.pallas_call(
        _ffn_down_kernel,
        grid=(m // bm,),
        in_specs=[
            pl.BlockSpec((bm, dff), lambda i: (i, 0)),
            pl.BlockSpec((dff, d), lambda i: (0, 0), pipeline_mode=pl.Buffered(1)),
            pl.BlockSpec((bm, d), lambda i: (i, 0)),
            pl.BlockSpec((1, d), lambda i: (0, 0)),
            pl.BlockSpec((1, d), lambda i: (0, 0)),
        ],
        out_specs=out_specs,
        out_shape=out_shape,
        compiler_params=_params(("arbitrary",), vmem),
        name="ffn_down_norm",
    )(act, w_down, x2, g_post.reshape(1, d), g_pre.reshape(1, d))
    return (outs[0], outs[1]) if with_h else (outs[0], None)


def kernel(x, rel_bias, w_in, b_f, w_pa, w_pb, w_o, w_up, conv_w, conv_b, w_down,
           g_mix_pre, g_mix_post, g_ffn_pre, g_ffn_post):
    b, t, d = x.shape
    depth = w_in.shape[0]
    m = b * t
    assert t % max(dil for _, dil in A_PATTERNS) == 0 and (t // A_PATTERNS[-1][1]) % Q_BLOCK == 0
    assert w_in.shape[2] == N_QKV + B_HEADS + 2 * d

    bias_tiles = _bias_tiles(rel_bias)
    wt_in = jnp.swapaxes(w_in, 1, 2)
    x2 = x.reshape(m, d)
    h = _pre_norm(x2, g_mix_pre[0])
    for layer in range(depth):
        bf_col = jnp.zeros((LANES, 1), F32).at[:B_HEADS, 0].set(b_f[layer])

        proj = _matmul_wt(h, wt_in, layer, 0, N_QKV, "proj_qkv")
        gates = _matmul_wt(h, wt_in, layer, N_QKV + B_HEADS, 2 * d, "proj_gates")
        c_row, c_col = _forget(h.reshape(b, t, d), wt_in, layer, N_QKV, bf_col)
        proj3 = proj.reshape(b, t, N_QKV)
        y_a = _dilated(proj3, bias_tiles)
        y_b = _fox(proj3, c_row, c_col)
        x2, h2 = _mix(y_a.reshape(m, -1), y_b.reshape(m, -1), gates,
                      w_pa[layer].astype(BF16), w_pb[layer].astype(BF16), w_o[layer].astype(BF16),
                      x2, g_mix_post[layer], g_ffn_pre[layer])
        act = _ffn_up(h2, w_up, layer, conv_w[layer], conv_b[layer], t)
        g_next = g_mix_pre[layer + 1] if layer + 1 < depth else None
        x2, h = _ffn_down(act, w_down[layer].astype(BF16), x2, g_ffn_post[layer], g_next)
    return x2.reshape(b, t, d)
```
